```python
import jax, jax.numpy as jnp
from jax import lax
import numpy as np

D_MODEL = 2048
BATCH = 16
SEQ = 2048
DEPTH = 4
DEC_BATCH = 4
DEC_SEQ = 2048
PAST_LEN = 128

N_MIXERS = 2
N_CONV_LAYERS = (DEPTH + 1) // 2
N_ATTN_LAYERS = DEPTH // 2
CONV_WIDTH = 3
HEAD_DIM = 128
N_HEADS = D_MODEL // HEAD_DIM
N_KV_HEADS = 4
GROUP = N_HEADS // N_KV_HEADS
WINDOW = 128
BLOCK = 128
NEIGH = WINDOW // BLOCK
BAND = (2 * NEIGH + 1) * BLOCK
ROPE_THETA = 10000.0
D_FF = ((8 * D_MODEL + 3 * 256 - 1) // (3 * 256)) * 256
EPS = 1e-6
NEG_INF = -1e30

kernel_name = "hybrid_shortconv_swa_sink_encoder"


def rmsnorm(x, g):
    xf = x.astype(jnp.float32)
    r = lax.rsqrt(jnp.mean(xf * xf, axis=-1, keepdims=True) + EPS)
    return (xf * r).astype(x.dtype) * g


def short_conv_mixer(x, w_in, w_conv, w_out):
    bch = x @ w_in
    b, c, h = jnp.split(bch, 3, axis=-1)
    u = c * h
    up = jnp.pad(u, ((0, 0), (CONV_WIDTH // 2, CONV_WIDTH // 2), (0, 0)))
    S = x.shape[1]
    v = sum(w_conv[t] * up[:, t:t + S] for t in range(CONV_WIDTH))
    return (b * v) @ w_out


def rope(x, cos, sin):
    x1, x2 = jnp.split(x, 2, axis=-1)
    return jnp.concatenate([x1 * cos - x2 * sin, x2 * cos + x1 * sin], axis=-1)


def band_mask(S):
    nb = S // BLOCK
    qpos = np.arange(nb)[:, None, None] * BLOCK + np.arange(BLOCK)[None, :, None]
    kpos = (np.arange(nb)[:, None, None] - NEIGH) * BLOCK + np.arange(BAND)[None, None, :]
    valid = (np.abs(qpos - kpos) <= WINDOW) & (kpos >= 0) & (kpos < S)
    return jnp.asarray(valid)


def windowed_gqa_sink(x, w_qkv, w_o, sink):
    B, S, _ = x.shape
    nb = S // BLOCK
    qkv = x @ w_qkv
    q, k, v = jnp.split(qkv, [N_HEADS * HEAD_DIM, (N_HEADS + N_KV_HEADS) * HEAD_DIM], axis=-1)
    q = q.reshape(B, S, N_HEADS, HEAD_DIM)
    k = k.reshape(B, S, N_KV_HEADS, HEAD_DIM)
    v = v.reshape(B, S, N_KV_HEADS, HEAD_DIM)

    inv_freq = 1.0 / (ROPE_THETA ** (jnp.arange(0, HEAD_DIM, 2, dtype=jnp.float32) / HEAD_DIM))
    ang = jnp.arange(S, dtype=jnp.float32)[:, None] * inv_freq[None, :]
    cos = jnp.cos(ang)[:, None, :].astype(x.dtype)
    sin = jnp.sin(ang)[:, None, :].astype(x.dtype)
    q = rope(q, cos, sin)
    k = rope(k, cos, sin)

    q = q.reshape(B, nb, BLOCK, N_KV_HEADS, GROUP, HEAD_DIM)
    pad = ((0, 0), (NEIGH * BLOCK, NEIGH * BLOCK), (0, 0), (0, 0))
    kp = jnp.pad(k, pad).reshape(B, nb + 2 * NEIGH, BLOCK, N_KV_HEADS, HEAD_DIM)
    vp = jnp.pad(v, pad).reshape(B, nb + 2 * NEIGH, BLOCK, N_KV_HEADS, HEAD_DIM)
    kb = jnp.concatenate([kp[:, t:t + nb] for t in range(2 * NEIGH + 1)], axis=2)
    vb = jnp.concatenate([vp[:, t:t + nb] for t in range(2 * NEIGH + 1)], axis=2)

    scale = HEAD_DIM ** -0.5
    s = jnp.einsum('bnqkgd,bnpkd->bnkgqp', q, kb).astype(jnp.float32) * scale
    mask = band_mask(S)[None, :, None, None]
    s = jnp.where(mask, s, NEG_INF)
    sink_b = jnp.broadcast_to(sink.astype(jnp.float32).reshape(N_KV_HEADS, GROUP)[None, None, :, :, None, None],
                              s.shape[:-1] + (1,))
    p = jax.nn.softmax(jnp.concatenate([s, sink_b], axis=-1), axis=-1)[..., :BAND].astype(v.dtype)
    o = jnp.einsum('bnkgqp,bnpkd->bnqkgd', p, vb).reshape(B, S, N_HEADS * HEAD_DIM)
    return o @ w_o


def swiglu(x, w_gate, w_up, w_down):
    return (jax.nn.silu(x @ w_gate) * (x @ w_up)) @ w_down


def trunk(x, conv_w_in, conv_w_dw, conv_w_out, attn_w_qkv, attn_w_o, attn_sink,
          ffn_w_gate, ffn_w_up, ffn_w_down, g_mix_pre, g_mix_post, g_ffn_pre, g_ffn_post):
    for i in range(DEPTH):
        h = rmsnorm(x, g_mix_pre[i])
        j = i // N_MIXERS
        if i % N_MIXERS == 0:
            h = short_conv_mixer(h, conv_w_in[j], conv_w_dw[j], conv_w_out[j])
        else:
            h = windowed_gqa_sink(h, attn_w_qkv[j], attn_w_o[j], attn_sink[j])
        x = x + rmsnorm(h, g_mix_post[i])
        h = swiglu(rmsnorm(x, g_ffn_pre[i]), ffn_w_gate[i], ffn_w_up[i], ffn_w_down[i])
        x = x + rmsnorm(h, g_ffn_post[i])
    return x


def setup_inputs(seed: int = 0) -> dict:
    key = jax.random.key(seed)
    ks = jax.random.split(key, 16)
    D, F = D_MODEL, D_FF
    QKV = (N_HEADS + 2 * N_KV_HEADS) * HEAD_DIM
    nrm = jax.random.normal
    f32 = jnp.float32
    return {
        "x_prompt": nrm(ks[0], (BATCH, SEQ, D), f32),
        "x_sample": nrm(ks[1], (DEC_BATCH, DEC_SEQ, D), f32),
        "conv_w_in": nrm(ks[2], (N_CONV_LAYERS, D, 3 * D), f32) * D ** -0.5,
        "conv_w_dw": nrm(ks[3], (N_CONV_LAYERS, CONV_WIDTH, D), f32) * CONV_WIDTH ** -0.5,
        "conv_w_out": nrm(ks[4], (N_CONV_LAYERS, D, D), f32) * D ** -0.5,
        "attn_w_qkv": nrm(ks[5], (N_ATTN_LAYERS, D, QKV), f32) * D ** -0.5,
        "attn_w_o": nrm(ks[6], (N_ATTN_LAYERS, N_HEADS * HEAD_DIM, D), f32) * (N_HEADS * HEAD_DIM) ** -0.5,
        "attn_sink": nrm(ks[7], (N_ATTN_LAYERS, N_HEADS), f32) * 0.5,
        "ffn_w_gate": nrm(ks[8], (DEPTH, D, F), f32) * D ** -0.5,
        "ffn_w_up": nrm(ks[9], (DEPTH, D, F), f32) * D ** -0.5,
        "ffn_w_down": nrm(ks[10], (DEPTH, F, D), f32) * F ** -0.5,
        "g_mix_pre": 1.0 + 0.05 * nrm(ks[11], (DEPTH, D), f32),
        "g_mix_post": 1.0 + 0.05 * nrm(ks[12], (DEPTH, D), f32),
        "g_ffn_pre": 1.0 + 0.05 * nrm(ks[13], (DEPTH, D), f32),
        "g_ffn_post": 1.0 + 0.05 * nrm(ks[14], (DEPTH, D), f32),
    }


def reference(x_prompt, x_sample, conv_w_in, conv_w_dw, conv_w_out, attn_w_qkv, attn_w_o,
              attn_sink, ffn_w_gate, ffn_w_up, ffn_w_down, g_mix_pre, g_mix_post, g_ffn_pre,
              g_ffn_post):
    y_prompt = trunk(x_prompt, conv_w_in, conv_w_dw, conv_w_out, attn_w_qkv, attn_w_o, attn_sink,
                     ffn_w_gate, ffn_w_up, ffn_w_down, g_mix_pre, g_mix_post, g_ffn_pre, g_ffn_post)
    y_sample = trunk(x_sample, conv_w_in, conv_w_dw, conv_w_out, attn_w_qkv, attn_w_o, attn_sink,
                     ffn_w_gate, ffn_w_up, ffn_w_down, g_mix_pre, g_mix_post, g_ffn_pre, g_ffn_post)
    return (y_prompt, y_sample)
```

```python
import functools

import jax
import jax.numpy as jnp
from jax import lax
from jax.experimental import pallas as pl
from jax.experimental.pallas import tpu as pltpu

EPS = 1e-6
NEG_INF = -1e30
ROPE_THETA = 10000.0
HEAD_DIM = 128
N_KV_HEADS = 4
WINDOW = 128
BLOCK = 128
BAND = 3 * BLOCK
CONV_WIDTH = 3
HALO = 16

BF16 = jnp.bfloat16
F32 = jnp.float32

VMEM_LIMIT_BYTES = 56 * 1024 * 1024


def _rms(x, g):
    r = lax.rsqrt(jnp.mean(x * x, axis=-1, keepdims=True) + EPS)
    return (x * r) * g


def _params(n_grid_dims):
    sem = ("parallel",) + ("arbitrary",) * (n_grid_dims - 1)
    return pltpu.CompilerParams(dimension_semantics=sem, vmem_limit_bytes=VMEM_LIMIT_BYTES)


def _accumulate_then_finish(o_ref, x_ref, gpost_ref, part, step, n_steps):
    @pl.when(step == 0)
    def _():
        o_ref[...] = part

    @pl.when(step > 0)
    def _():
        o_ref[...] += part

    @pl.when(step == n_steps - 1)
    def _():
        o_ref[...] = x_ref[...] + _rms(o_ref[...], gpost_ref[...])


def _ffn_kernel(x_ref, gpre_ref, wg_ref, wu_ref, wd_ref, gpost_ref, o_ref, hn_ref):
    f = pl.program_id(1)

    @pl.when(f == 0)
    def _():
        hn_ref[...] = _rms(x_ref[...], gpre_ref[...]).astype(BF16)

    hn = hn_ref[...]
    gate = jnp.dot(hn, wg_ref[...], preferred_element_type=F32)
    up = jnp.dot(hn, wu_ref[...], preferred_element_type=F32)
    act = (jax.nn.silu(gate) * up).astype(BF16)
    part = jnp.dot(act, wd_ref[...], preferred_element_type=F32)
    _accumulate_then_finish(o_ref, x_ref, gpost_ref, part, f, pl.num_programs(1))


def _ffn(x, g_pre, w_gate, w_up, w_down, g_post, *, tm, tf):
    n, d = x.shape
    f = w_gate.shape[1]
    return pl.pallas_call(
        _ffn_kernel,
        name="ffn",
        grid=(n // tm, f // tf),
        in_specs=[
            pl.BlockSpec((tm, d), lambda i, j: (i, 0)),
            pl.BlockSpec((1, d), lambda i, j: (0, 0)),
            pl.BlockSpec((d, tf), lambda i, j: (0, j)),
            pl.BlockSpec((d, tf), lambda i, j: (0, j)),
            pl.BlockSpec((tf, d), lambda i, j: (j, 0)),
            pl.BlockSpec((1, d), lambda i, j: (0, 0)),
        ],
        out_specs=pl.BlockSpec((tm, d), lambda i, j: (i, 0)),
        out_shape=jax.ShapeDtypeStruct((n, d), F32),
        scratch_shapes=[pltpu.VMEM((tm, d), BF16)],
        compiler_params=_params(2),
    )(x, g_pre, w_gate, w_up, w_down, g_post)


def _conv_kernel(xm_ref, xp_ref, xn_ref, gpre_ref, wb_ref, wc_ref, wh_ref, wdw_ref, wo_ref,
                 gpost_ref, o_ref, hn_ref, *, tiles_per_seq):
    i = pl.program_id(0)
    j = pl.program_id(1)
    tm = xm_ref.shape[0]

    @pl.when(j == 0)
    def _():
        g = gpre_ref[...]
        hn_ref[HALO:HALO + tm, :] = _rms(xm_ref[...], g).astype(BF16)
        t = i % tiles_per_seq
        keep_prev = (t != 0).astype(F32)
        keep_next = (t != tiles_per_seq - 1).astype(F32)
        hn_ref[0:HALO, :] = (_rms(xp_ref[...], g) * keep_prev).astype(BF16)
        hn_ref[HALO + tm:, :] = (_rms(xn_ref[...], g) * keep_next).astype(BF16)

    hn = hn_ref[...]
    c = jnp.dot(hn, wc_ref[...], preferred_element_type=F32)
    h = jnp.dot(hn, wh_ref[...], preferred_element_type=F32)
    u = c * h
    rows = tm + 2 * HALO
    u_prev = pltpu.roll(u, 1, 0)[HALO:HALO + tm]
    u_next = pltpu.roll(u, rows - 1, 0)[HALO:HALO + tm]
    w = wdw_ref[...]
    v = w[0:1] * u_prev + w[1:2] * u[HALO:HALO + tm] + w[2:3] * u_next
    b = jnp.dot(hn_ref[HALO:HALO + tm, :], wb_ref[...], preferred_element_type=F32)
    y = (b * v).astype(BF16)
    part = jnp.dot(y, wo_ref[...], preferred_element_type=F32)
    _accumulate_then_finish(o_ref, xm_ref, gpost_ref, part, j, pl.num_programs(1))


def _conv_mixer(x, g_pre, w_in, w_dw, w_out, g_post, *, seq, tm, tc):
    n, d = x.shape
    nc = d // tc
    hb = tm // HALO
    n_hb = n // HALO
    kern = functools.partial(_conv_kernel, tiles_per_seq=seq // tm)
    return pl.pallas_call(
        kern,
        name="conv_mixer",
        grid=(n // tm, nc),
        in_specs=[
            pl.BlockSpec((tm, d), lambda i, j: (i, 0)),
            pl.BlockSpec((HALO, d), lambda i, j: (jnp.maximum(i * hb - 1, 0), 0)),
            pl.BlockSpec((HALO, d), lambda i, j: (jnp.minimum((i + 1) * hb, n_hb - 1), 0)),
            pl.BlockSpec((1, d), lambda i, j: (0, 0)),
            pl.BlockSpec((d, tc), lambda i, j: (0, j)),
            pl.BlockSpec((d, tc), lambda i, j: (0, nc + j)),
            pl.BlockSpec((d, tc), lambda i, j: (0, 2 * nc + j)),
            pl.BlockSpec((CONV_WIDTH, tc), lambda i, j: (0, j)),
            pl.BlockSpec((tc, d), lambda i, j: (j, 0)),
            pl.BlockSpec((1, d), lambda i, j: (0, 0)),
        ],
        out_specs=pl.BlockSpec((tm, d), lambda i, j: (i, 0)),
        out_shape=jax.ShapeDtypeStruct((n, d), F32),
        scratch_shapes=[pltpu.VMEM((tm + 2 * HALO, d), BF16)],
        compiler_params=_params(2),
    )(x, x, x, g_pre, w_in, w_in, w_in, w_dw, w_out, g_post)


def _qkv_kernel(x_ref, gpre_ref, w_ref, cos_ref, sin_ref, q_ref, k_ref, v_ref):
    dq = q_ref.shape[1]
    dk = k_ref.shape[1]
    hn = _rms(x_ref[...], gpre_ref[...]).astype(BF16)
    qkv = jnp.dot(hn, w_ref[...], preferred_element_type=F32)
    cos = cos_ref[...]
    sin = sin_ref[...]

    def rope(xh):
        return xh * cos + pltpu.roll(xh, HEAD_DIM // 2, 1) * sin

    for hd in range(dq // HEAD_DIM):
        sl = slice(hd * HEAD_DIM, (hd + 1) * HEAD_DIM)
        q_ref[:, sl] = rope(qkv[:, sl]).astype(BF16)
    for hd in range(dk // HEAD_DIM):
        sl = slice(hd * HEAD_DIM, (hd + 1) * HEAD_DIM)
        k_ref[:, sl] = rope(qkv[:, dq + hd * HEAD_DIM:dq + (hd + 1) * HEAD_DIM]).astype(BF16)
    v_ref[...] = qkv[:, dq + dk:].astype(BF16)


def _qkv_rope(x, g_pre, w_qkv, cos, sin, *, seq, tm):
    n, d = x.shape
    dqkv = w_qkv.shape[1]
    dk = N_KV_HEADS * HEAD_DIM
    dq = dqkv - 2 * dk
    tps = seq // tm
    return pl.pallas_call(
        _qkv_kernel,
        name="qkv_rope",
        grid=(n // tm,),
        in_specs=[
            pl.BlockSpec((tm, d), lambda i: (i, 0)),
            pl.BlockSpec((1, d), lambda i: (0, 0)),
            pl.BlockSpec((d, dqkv), lambda i: (0, 0), pipeline_mode=pl.Buffered(1)),
            pl.BlockSpec((tm, HEAD_DIM), lambda i: (i % tps, 0)),
            pl.BlockSpec((tm, HEAD_DIM), lambda i: (i % tps, 0)),
        ],
        out_specs=[
            pl.BlockSpec((tm, dq), lambda i: (i, 0)),
            pl.BlockSpec((tm, dk), lambda i: (i, 0)),
            pl.BlockSpec((tm, dk), lambda i: (i, 0)),
        ],
        out_shape=[
            jax.ShapeDtypeStruct((n, dq), BF16),
            jax.ShapeDtypeStruct((n, dk), BF16),
            jax.ShapeDtypeStruct((n, dk), BF16),
        ],
        compiler_params=_params(1),
    )(x, g_pre, w_qkv, cos, sin)


def _attn_kernel(sink_ref, x_ref, q_ref, k_ref, v_ref, wo_ref, gpost_ref, o_ref, att_ref, *, seq):
    qi = pl.program_id(1)
    tq = q_ref.shape[0]
    n_heads = q_ref.shape[1] // HEAD_DIM
    group = n_heads // N_KV_HEADS
    scale = HEAD_DIM ** -0.5
    row_minus_col = (lax.broadcasted_iota(jnp.int32, (BLOCK, BAND), 0)
                     - lax.broadcasted_iota(jnp.int32, (BLOCK, BAND), 1))

    def q_block(qb, carry):
        q0 = pl.multiple_of(qb * BLOCK, BLOCK)
        qpos0 = qi * tq + q0
        win0 = pl.multiple_of(jnp.clip(qpos0 - BLOCK, 0, seq - BAND), BLOCK)
        diff = row_minus_col + (qpos0 - win0)
        valid = (diff <= WINDOW) & (diff >= -WINDOW)
        for kv in range(N_KV_HEADS):
            ksl = slice(kv * HEAD_DIM, (kv + 1) * HEAD_DIM)
            kw = k_ref[pl.ds(win0, BAND), ksl]
            vw = v_ref[pl.ds(win0, BAND), ksl]
            qg = jnp.concatenate(
                [q_ref[pl.ds(q0, BLOCK), (kv * group + g) * HEAD_DIM:(kv * group + g + 1) * HEAD_DIM]
                 for g in range(group)], axis=0)
            s = lax.dot_general(qg, kw, (((1,), (1,)), ((), ())), preferred_element_type=F32)
            s = (s * scale).reshape(group, BLOCK, BAND)
            s = jnp.where(valid[None], s, NEG_INF)
            sink = jnp.stack([jnp.full((BLOCK, 1), sink_ref[kv * group + g], F32)
                              for g in range(group)], axis=0)
            m = jnp.maximum(jnp.max(s, axis=-1, keepdims=True), sink)
            p = jnp.exp(s - m)
            denom = jnp.sum(p, axis=-1, keepdims=True) + jnp.exp(sink - m)
            p = (p / denom).astype(BF16).reshape(group * BLOCK, BAND)
            og = jnp.dot(p, vw, preferred_element_type=F32)
            for g in range(group):
                hd = kv * group + g
                att_ref[pl.ds(q0, BLOCK), hd * HEAD_DIM:(hd + 1) * HEAD_DIM] = (
                    og[g * BLOCK:(g + 1) * BLOCK].astype(BF16))
        return carry

    lax.fori_loop(0, tq // BLOCK, q_block, 0)
    y = jnp.dot(att_ref[...], wo_ref[...], preferred_element_type=F32)
    o_ref[...] = x_ref[...] + _rms(y, gpost_ref[...])


def _attention(x, q, k, v, w_o, sink, g_post, *, seq, tq):
    n, d = x.shape
    dq = q.shape[1]
    dk = k.shape[1]
    tps = seq // tq
    kern = functools.partial(_attn_kernel, seq=seq)
    grid_spec = pltpu.PrefetchScalarGridSpec(
        num_scalar_prefetch=1,
        grid=(n // seq, tps),
        in_specs=[
            pl.BlockSpec((tq, d), lambda b, i, s: (b * tps + i, 0)),
            pl.BlockSpec((tq, dq), lambda b, i, s: (b * tps + i, 0)),
            pl.BlockSpec((seq, dk), lambda b, i, s: (b, 0)),
            pl.BlockSpec((seq, dk), lambda b, i, s: (b, 0)),
            pl.BlockSpec((dq, d), lambda b, i, s: (0, 0), pipeline_mode=pl.Buffered(1)),
            pl.BlockSpec((1, d), lambda b, i, s: (0, 0)),
        ],
        out_specs=pl.BlockSpec((tq, d), lambda b, i, s: (b * tps + i, 0)),
        scratch_shapes=[pltpu.VMEM((tq, dq), BF16)],
    )
    return pl.pallas_call(
        kern,
        name="attention",
        grid_spec=grid_spec,
        out_shape=jax.ShapeDtypeStruct((n, d), F32),
        compiler_params=_params(2),
    )(sink, x, q, k, v, w_o, g_post)


def _rope_tables(seq):
    inv_freq = 1.0 / (ROPE_THETA ** (jnp.arange(0, HEAD_DIM, 2, dtype=F32) / HEAD_DIM))
    ang = jnp.arange(seq, dtype=F32)[:, None] * inv_freq[None, :]
    cos = jnp.cos(ang)
    sin = jnp.sin(ang)
    return jnp.concatenate([cos, cos], axis=-1), jnp.concatenate([-sin, sin], axis=-1)


def _trunk(x3, w, *, tm_ffn, tf, tm_conv, tc, tm_qkv, tq):
    b, seq, d = x3.shape
    x = x3.reshape(b * seq, d)
    depth = w["ffn_w_gate"].shape[0]
    cos, sin = _rope_tables(seq)
    for i in range(depth):
        j = i // 2
        g_pre = w["g_mix_pre"][i][None]
        g_post = w["g_mix_post"][i][None]
        if i % 2 == 0:
            x = _conv_mixer(x, g_pre, w["conv_w_in"][j], w["conv_w_dw"][j], w["conv_w_out"][j],
                            g_post, seq=seq, tm=tm_conv, tc=tc)
        else:
            q, k, v = _qkv_rope(x, g_pre, w["attn_w_qkv"][j], cos, sin, seq=seq, tm=tm_qkv)
            x = _attention(x, q, k, v, w["attn_w_o"][j], w["attn_sink"][j], g_post, seq=seq, tq=tq)
        x = _ffn(x, w["g_ffn_pre"][i][None], w["ffn_w_gate"][i], w["ffn_w_up"][i],
                 w["ffn_w_down"][i], w["g_ffn_post"][i][None], tm=tm_ffn, tf=tf)
    return x.reshape(b, seq, d)


def kernel(x_prompt, x_sample, conv_w_in, conv_w_dw, conv_w_out, attn_w_qkv, attn_w_o, attn_sink,
           ffn_w_gate, ffn_w_up, ffn_w_down, g_mix_pre, g_mix_post, g_ffn_pre, g_ffn_post):
    w = dict(
        conv_w_in=conv_w_in.astype(BF16), conv_w_dw=conv_w_dw, conv_w_out=conv_w_out.astype(BF16),
        attn_w_qkv=attn_w_qkv.astype(BF16), attn_w_o=attn_w_o.astype(BF16), attn_sink=attn_sink,
        ffn_w_gate=ffn_w_gate.astype(BF16), ffn_w_up=ffn_w_up.astype(BF16),
        ffn_w_down=ffn_w_down.astype(BF16),
        g_mix_pre=g_mix_pre, g_mix_post=g_mix_post, g_ffn_pre=g_ffn_pre, g_ffn_post=g_ffn_post,
    )
    tiles = dict(tm_ffn=512, tf=512, tm_conv=512, tc=512, tm_qkv=512, tq=512)
    return _trunk(x_prompt, w, **tiles), _trunk(x_sample, w, **tiles)
```

```python
import functools

import jax
import jax.numpy as jnp
from jax import lax
from jax.experimental import pallas as pl
from jax.experimental.pallas import tpu as pltpu

EPS = 1e-6
NEG_INF = -1e30
LOG2E = 1.4426950408889634
ROPE_THETA = 10000.0
HEAD_DIM = 128
N_KV_HEADS = 4
WINDOW = 128
BLOCK = 128
BAND = 3 * BLOCK
CONV_WIDTH = 3
HALO = 16

BF16 = jnp.bfloat16
F32 = jnp.float32

VMEM_LIMIT_BYTES = 62 * 1024 * 1024


def _rms(x, g):
    r = lax.rsqrt(jnp.mean(x * x, axis=-1, keepdims=True) + EPS)
    return (x * r) * g


ROW_CHUNK = 128


def _for_row_chunks(n_rows, body):
    chunk = min(ROW_CHUNK, n_rows)

    def step(c, carry):
        body(pl.multiple_of(c * chunk, chunk), chunk)
        return carry

    lax.fori_loop(0, n_rows // chunk, step, 0, unroll=2)


def _prenorm_rows(dst_ref, dst_row0, src_ref, g_ref, n_rows):
    def body(r0, chunk):
        out = _rms(src_ref[pl.ds(r0, chunk), :], g_ref[...]).astype(dst_ref.dtype)
        dst_ref[pl.ds(pl.multiple_of(dst_row0 + r0, HALO), chunk), :] = out

    _for_row_chunks(n_rows, body)


def _postnorm_residual_rows(o_ref, x_ref, g_ref):
    def body(r0, chunk):
        rows = pl.ds(r0, chunk)
        o_ref[rows, :] = x_ref[rows, :] + _rms(o_ref[rows, :], g_ref[...])

    _for_row_chunks(o_ref.shape[0], body)


def _params(n_grid_dims):
    sem = ("parallel",) + ("arbitrary",) * (n_grid_dims - 1)
    return pltpu.CompilerParams(dimension_semantics=sem, vmem_limit_bytes=VMEM_LIMIT_BYTES)


def _ffn_kernel(x_ref, gpre_ref, wg_ref, wu_ref, wd_ref, gpost_ref, o_ref, hn_ref):
    f = pl.program_id(1)

    @pl.when(f == 0)
    def _():
        _prenorm_rows(hn_ref, 0, x_ref, gpre_ref, x_ref.shape[0])
        o_ref[...] = jnp.zeros_like(o_ref)

    hn = hn_ref[...]
    gate = jnp.dot(hn, wg_ref[...], preferred_element_type=F32)
    up = jnp.dot(hn, wu_ref[...], preferred_element_type=F32)
    act = (jax.nn.silu(gate) * up).astype(BF16)
    o_ref[...] += jnp.dot(act, wd_ref[...], preferred_element_type=F32)

    @pl.when(f == pl.num_programs(1) - 1)
    def _():
        _postnorm_residual_rows(o_ref, x_ref, gpost_ref)


def _ffn(x, g_pre, w_gate, w_up, w_down, g_post, *, tm, tf):
    n, d = x.shape
    f = w_gate.shape[1]
    return pl.pallas_call(
        _ffn_kernel,
        name="ffn",
        grid=(n // tm, f // tf),
        in_specs=[
            pl.BlockSpec((tm, d), lambda i, j: (i, 0)),
            pl.BlockSpec((1, d), lambda i, j: (0, 0)),
            pl.BlockSpec((d, tf), lambda i, j: (0, j)),
            pl.BlockSpec((d, tf), lambda i, j: (0, j)),
            pl.BlockSpec((tf, d), lambda i, j: (j, 0)),
            pl.BlockSpec((1, d), lambda i, j: (0, 0)),
        ],
        out_specs=pl.BlockSpec((tm, d), lambda i, j: (i, 0)),
        out_shape=jax.ShapeDtypeStruct((n, d), F32),
        scratch_shapes=[pltpu.VMEM((tm, d), BF16)],
        compiler_params=_params(2),
    )(x, g_pre, w_gate, w_up, w_down, g_post)


def _conv_kernel(xm_ref, xp_ref, xn_ref, gpre_ref, wb_ref, wc_ref, wh_ref, wdw_ref, wo_ref,
                 gpost_ref, o_ref, hn_ref, *, tiles_per_seq):
    i = pl.program_id(0)
    j = pl.program_id(1)
    tm = xm_ref.shape[0]

    @pl.when(j == 0)
    def _():
        g = gpre_ref[...]
        _prenorm_rows(hn_ref, HALO, xm_ref, gpre_ref, tm)
        t = i % tiles_per_seq
        keep_prev = (t != 0).astype(F32)
        keep_next = (t != tiles_per_seq - 1).astype(F32)
        hn_ref[0:HALO, :] = (_rms(xp_ref[...], g) * keep_prev).astype(BF16)
        hn_ref[HALO + tm:, :] = (_rms(xn_ref[...], g) * keep_next).astype(BF16)
        o_ref[...] = jnp.zeros_like(o_ref)

    hn = hn_ref[...]
    c = jnp.dot(hn, wc_ref[...], preferred_element_type=F32)
    h = jnp.dot(hn, wh_ref[...], preferred_element_type=F32)
    u = c * h
    rows = tm + 2 * HALO
    u_prev = pltpu.roll(u, 1, 0)[HALO:HALO + tm]
    u_next = pltpu.roll(u, rows - 1, 0)[HALO:HALO + tm]
    w = wdw_ref[...]
    v = w[0:1] * u_prev + w[1:2] * u[HALO:HALO + tm] + w[2:3] * u_next
    b = jnp.dot(hn_ref[HALO:HALO + tm, :], wb_ref[...], preferred_element_type=F32)
    y = (b * v).astype(BF16)
    o_ref[...] += jnp.dot(y, wo_ref[...], preferred_element_type=F32)

    @pl.when(j == pl.num_programs(1) - 1)
    def _():
        _postnorm_residual_rows(o_ref, xm_ref, gpost_ref)


def _conv_mixer(x, g_pre, w_in, w_dw, w_out, g_post, *, seq, tm, tc):
    n, d = x.shape
    nc = d // tc
    hb = tm // HALO
    n_hb = n // HALO
    kern = functools.partial(_conv_kernel, tiles_per_seq=seq // tm)
    return pl.pallas_call(
        kern,
        name="conv_mixer",
        grid=(n // tm, nc),
        in_specs=[
            pl.BlockSpec((tm, d), lambda i, j: (i, 0)),
            pl.BlockSpec((HALO, d), lambda i, j: (jnp.maximum(i * hb - 1, 0), 0)),
            pl.BlockSpec((HALO, d), lambda i, j: (jnp.minimum((i + 1) * hb, n_hb - 1), 0)),
            pl.BlockSpec((1, d), lambda i, j: (0, 0)),
            pl.BlockSpec((d, tc), lambda i, j: (0, j)),
            pl.BlockSpec((d, tc), lambda i, j: (0, nc + j)),
            pl.BlockSpec((d, tc), lambda i, j: (0, 2 * nc + j)),
            pl.BlockSpec((CONV_WIDTH, tc), lambda i, j: (0, j)),
            pl.BlockSpec((tc, d), lambda i, j: (j, 0)),
            pl.BlockSpec((1, d), lambda i, j: (0, 0)),
        ],
        out_specs=pl.BlockSpec((tm, d), lambda i, j: (i, 0)),
        out_shape=jax.ShapeDtypeStruct((n, d), F32),
        scratch_shapes=[pltpu.VMEM((tm + 2 * HALO, d), BF16)],
        compiler_params=_params(2),
    )(x, x, x, g_pre, w_in, w_in, w_in, w_dw, w_out, g_post)


def _qkv_kernel(x_ref, gpre_ref, w_ref, cos_ref, sin_ref, q_ref, k_ref, v_ref):
    dq = q_ref.shape[1]
    dk = k_ref.shape[1]
    hn = _rms(x_ref[...], gpre_ref[...]).astype(BF16)
    qkv = jnp.dot(hn, w_ref[...], preferred_element_type=F32)
    cos = cos_ref[...]
    sin = sin_ref[...]

    def rope(xh):
        return xh * cos + pltpu.roll(xh, HEAD_DIM // 2, 1) * sin

    for hd in range(dq // HEAD_DIM):
        sl = slice(hd * HEAD_DIM, (hd + 1) * HEAD_DIM)
        q_ref[:, sl] = rope(qkv[:, sl]).astype(BF16)
    for hd in range(dk // HEAD_DIM):
        sl = slice(hd * HEAD_DIM, (hd + 1) * HEAD_DIM)
        k_ref[:, sl] = rope(qkv[:, dq + hd * HEAD_DIM:dq + (hd + 1) * HEAD_DIM]).astype(BF16)
    v_ref[...] = qkv[:, dq + dk:].astype(BF16)


def _qkv_rope(x, g_pre, w_qkv, cos, sin, *, seq, tm):
    n, d = x.shape
    dqkv = w_qkv.shape[1]
    dk = N_KV_HEADS * HEAD_DIM
    dq = dqkv - 2 * dk
    tps = seq // tm
    return pl.pallas_call(
        _qkv_kernel,
        name="qkv_rope",
        grid=(n // tm,),
        in_specs=[
            pl.BlockSpec((tm, d), lambda i: (i, 0)),
            pl.BlockSpec((1, d), lambda i: (0, 0)),
            pl.BlockSpec((d, dqkv), lambda i: (0, 0), pipeline_mode=pl.Buffered(1)),
            pl.BlockSpec((tm, HEAD_DIM), lambda i: (i % tps, 0)),
            pl.BlockSpec((tm, HEAD_DIM), lambda i: (i % tps, 0)),
        ],
        out_specs=[
            pl.BlockSpec((tm, dq), lambda i: (i, 0)),
            pl.BlockSpec((tm, dk), lambda i: (i, 0)),
            pl.BlockSpec((tm, dk), lambda i: (i, 0)),
        ],
        out_shape=[
            jax.ShapeDtypeStruct((n, dq), BF16),
            jax.ShapeDtypeStruct((n, dk), BF16),
            jax.ShapeDtypeStruct((n, dk), BF16),
        ],
        compiler_params=_params(1),
    )(x, g_pre, w_qkv, cos, sin)


def _attn_kernel(sink_ref, x_ref, q_ref, k_ref, v_ref, wo_ref, gpost_ref, o_ref, att_ref, *, seq):
    qi = pl.program_id(1)
    tq = q_ref.shape[0]
    n_heads = q_ref.shape[1] // HEAD_DIM
    group = n_heads // N_KV_HEADS
    exp_coef = HEAD_DIM ** -0.5 * LOG2E
    inv_scale = HEAD_DIM ** 0.5
    row_minus_col = (lax.broadcasted_iota(jnp.int32, (BLOCK, BAND), 0)
                     - lax.broadcasted_iota(jnp.int32, (BLOCK, BAND), 1))
    ones_cols = jnp.ones((BAND, HEAD_DIM), BF16)
    sink_row = ((lax.broadcasted_iota(jnp.int32, (BLOCK, 2 * HEAD_DIM), 0) == 0)
                & (lax.broadcasted_iota(jnp.int32, (BLOCK, 2 * HEAD_DIM), 1) >= HEAD_DIM))
    sink_rows = jnp.where(sink_row, 1.0, 0.0).astype(BF16)

    def window(qb):
        qpos0 = qi * tq + qb * BLOCK
        win0 = pl.multiple_of(jnp.clip(qpos0 - BLOCK, 0, seq - BAND), BLOCK)
        diff = row_minus_col + (qpos0 - win0)
        return win0, (diff <= WINDOW) & (diff >= -WINDOW)

    def scores(qb, kv, win0):
        rows = slice(qb * BLOCK, (qb + 1) * BLOCK)
        kw = k_ref[pl.ds(win0, BAND), kv * HEAD_DIM:(kv + 1) * HEAD_DIM]
        qg = jnp.concatenate(
            [q_ref[rows, (kv * group + g) * HEAD_DIM:(kv * group + g + 1) * HEAD_DIM]
             for g in range(group)], axis=0)
        return lax.dot_general(qg, kw, (((1,), (1,)), ((), ())), preferred_element_type=F32)

    def finish(qb, kv, s, win0, valid):
        rows = slice(qb * BLOCK, (qb + 1) * BLOCK)
        s = jnp.where(valid[None], s.reshape(group, BLOCK, BAND), NEG_INF)
        sink = jnp.concatenate(
            [jnp.full((1, BLOCK, BLOCK), sink_ref[kv * group + g] * inv_scale, F32)
             for g in range(group)], axis=0)
        s = jnp.concatenate([s, sink], axis=-1)
        m = jnp.max(s, axis=-1, keepdims=True)
        p = jnp.exp2((s - m) * exp_coef).astype(BF16).reshape(group * BLOCK, BAND + BLOCK)
        vw = v_ref[pl.ds(win0, BAND), kv * HEAD_DIM:(kv + 1) * HEAD_DIM]
        v_ext = jnp.concatenate([jnp.concatenate([vw, ones_cols], axis=1), sink_rows], axis=0)
        r = jnp.dot(p, v_ext, preferred_element_type=F32)
        og = r[:, :HEAD_DIM] / r[:, HEAD_DIM:]
        for g in range(group):
            hd = kv * group + g
            att_ref[rows, hd * HEAD_DIM:(hd + 1) * HEAD_DIM] = og[g * BLOCK:(g + 1) * BLOCK].astype(BF16)

    steps = [(qb, kv) for qb in range(tq // BLOCK) for kv in range(N_KV_HEADS)]
    win = {qb: window(qb) for qb in range(tq // BLOCK)}
    s_cur = scores(*steps[0], win[steps[0][0]][0])
    for idx, (qb, kv) in enumerate(steps):
        if idx + 1 < len(steps):
            nqb, nkv = steps[idx + 1]
            s_next = scores(nqb, nkv, win[nqb][0])
        finish(qb, kv, s_cur, *win[qb])
        s_cur = s_next
    o_ref[...] = jnp.dot(att_ref[...], wo_ref[...], preferred_element_type=F32)
    _postnorm_residual_rows(o_ref, x_ref, gpost_ref)


def _attention(x, q, k, v, w_o, sink, g_post, *, seq, tq):
    n, d = x.shape
    dq = q.shape[1]
    dk = k.shape[1]
    tps = seq // tq
    kern = functools.partial(_attn_kernel, seq=seq)
    grid_spec = pltpu.PrefetchScalarGridSpec(
        num_scalar_prefetch=1,
        grid=(n // seq, tps),
        in_specs=[
            pl.BlockSpec((tq, d), lambda b, i, s: (b * tps + i, 0)),
            pl.BlockSpec((tq, dq), lambda b, i, s: (b * tps + i, 0)),
            pl.BlockSpec((seq, dk), lambda b, i, s: (b, 0)),
            pl.BlockSpec((seq, dk), lambda b, i, s: (b, 0)),
            pl.BlockSpec((dq, d), lambda b, i, s: (0, 0), pipeline_mode=pl.Buffered(1)),
            pl.BlockSpec((1, d), lambda b, i, s: (0, 0)),
        ],
        out_specs=pl.BlockSpec((tq, d), lambda b, i, s: (b * tps + i, 0)),
        scratch_shapes=[pltpu.VMEM((tq, dq), BF16)],
    )
    return pl.pallas_call(
        kern,
        name="attention",
        grid_spec=grid_spec,
        out_shape=jax.ShapeDtypeStruct((n, d), F32),
        compiler_params=_params(2),
    )(sink, x, q, k, v, w_o, g_post)


def _rope_tables(seq):
    inv_freq = 1.0 / (ROPE_THETA ** (jnp.arange(0, HEAD_DIM, 2, dtype=F32) / HEAD_DIM))
    ang = jnp.arange(seq, dtype=F32)[:, None] * inv_freq[None, :]
    cos = jnp.cos(ang)
    sin = jnp.sin(ang)
    return jnp.concatenate([cos, cos], axis=-1), jnp.concatenate([-sin, sin], axis=-1)


def _trunk(x3, w, *, tm_ffn, tf, tm_conv, tc, tm_qkv, tq):
    b, seq, d = x3.shape
    x = x3.reshape(b * seq, d)
    depth = w["ffn_w_gate"].shape[0]
    cos, sin = _rope_tables(seq)
    for i in range(depth):
        j = i // 2
        g_pre = w["g_mix_pre"][i][None]
        g_post = w["g_mix_post"][i][None]
        if i % 2 == 0:
            x = _conv_mixer(x, g_pre, w["conv_w_in"][j], w["conv_w_dw"][j], w["conv_w_out"][j],
                            g_post, seq=seq, tm=tm_conv, tc=tc)
        else:
            q, k, v = _qkv_rope(x, g_pre, w["attn_w_qkv"][j], cos, sin, seq=seq, tm=tm_qkv)
            x = _attention(x, q, k, v, w["attn_w_o"][j], w["attn_sink"][j], g_post, seq=seq, tq=tq)
        x = _ffn(x, w["g_ffn_pre"][i][None], w["ffn_w_gate"][i], w["ffn_w_up"][i],
                 w["ffn_w_down"][i], w["g_ffn_post"][i][None], tm=tm_ffn, tf=tf)
    return x.reshape(b, seq, d)


def kernel(x_prompt, x_sample, conv_w_in, conv_w_dw, conv_w_out, attn_w_qkv, attn_w_o, attn_sink,
           ffn_w_gate, ffn_w_up, ffn_w_down, g_mix_pre, g_mix_post, g_ffn_pre, g_ffn_post):
    w = dict(
        conv_w_in=conv_w_in.astype(BF16), conv_w_dw=conv_w_dw, conv_w_out=conv_w_out.astype(BF16),
        attn_w_qkv=attn_w_qkv.astype(BF16), attn_w_o=attn_w_o.astype(BF16), attn_sink=attn_sink,
        ffn_w_gate=ffn_w_gate.astype(BF16), ffn_w_up=ffn_w_up.astype(BF16),
        ffn_w_down=ffn_w_down.astype(BF16),
        g_mix_pre=g_mix_pre, g_mix_post=g_mix_post, g_ffn_pre=g_ffn_pre, g_ffn_post=g_ffn_post,
    )
    tiles = dict(tm_ffn=1024, tf=512, tm_conv=512, tc=512, tm_qkv=512, tq=512)
    return _trunk(x_prompt, w, **tiles), _trunk(x_sample, w, **tiles)
```

```python
import functools

import jax
import jax.numpy as jnp
from jax import lax
from jax.experimental import pallas as pl
from jax.experimental.pallas import tpu as pltpu

EPS = 1e-6
NEG_INF = -1e30
LOG2E = 1.4426950408889634
ROPE_THETA = 10000.0
HEAD_DIM = 128
N_KV_HEADS = 4
WINDOW = 128
BLOCK = 128
BAND = 3 * BLOCK
CONV_WIDTH = 3
HALO = 16

BF16 = jnp.bfloat16
F32 = jnp.float32

VMEM_LIMIT_BYTES = 62 * 1024 * 1024


def _rms(x, g):
    r = lax.rsqrt(jnp.mean(x * x, axis=-1, keepdims=True) + EPS)
    return (x * r) * g


ROW_CHUNK = 128
FFN_ROW_BLOCK = 256
CONV_ROW_BLOCK = 256
ATTN_ROW_BLOCK = 256


def _for_row_chunks(n_rows, body):
    chunk = min(ROW_CHUNK, n_rows)

    def step(c, carry):
        body(pl.multiple_of(c * chunk, chunk), chunk)
        return carry

    lax.fori_loop(0, n_rows // chunk, step, 0, unroll=2)


def _prenorm_rows(dst_ref, dst_row0, src_ref, g_ref, n_rows):
    def body(r0, chunk):
        out = _rms(src_ref[pl.ds(r0, chunk), :], g_ref[...]).astype(dst_ref.dtype)
        dst_ref[pl.ds(pl.multiple_of(dst_row0 + r0, HALO), chunk), :] = out

    _for_row_chunks(n_rows, body)


def _postnorm_residual_rows(o_ref, x_ref, g_ref):
    def body(r0, chunk):
        rows = pl.ds(r0, chunk)
        o_ref[rows, :] = x_ref[rows, :] + _rms(o_ref[rows, :], g_ref[...])

    _for_row_chunks(o_ref.shape[0], body)


def _params(n_grid_dims):
    sem = ("parallel",) + ("arbitrary",) * (n_grid_dims - 1)
    return pltpu.CompilerParams(dimension_semantics=sem, vmem_limit_bytes=VMEM_LIMIT_BYTES)


def _pipelined(n_blocks, stage_a, stage_b):
    cur = stage_a(0)
    for i in range(n_blocks):
        nxt = stage_a(i + 1) if i + 1 < n_blocks else None
        stage_b(i, cur)
        cur = nxt


def _ffn_step(x_ref, gpre_ref, wg_ref, wu_ref, wd_ref, gpost_ref, o_ref, hn_ref, *, first, last,
              row_block):
    def rows(rb):
        return slice(rb * row_block, (rb + 1) * row_block)

    def gate_up(rb):
        if first:
            hn_ref[rows(rb), :] = _rms(x_ref[rows(rb), :], gpre_ref[...]).astype(BF16)
        hn = hn_ref[rows(rb), :]
        return (jnp.dot(hn, wg_ref[...], preferred_element_type=F32),
                jnp.dot(hn, wu_ref[...], preferred_element_type=F32))

    def down(rb, gate_and_up):
        gate, up = gate_and_up
        act = (jax.nn.silu(gate) * up).astype(BF16)
        part = jnp.dot(act, wd_ref[...], preferred_element_type=F32)
        if not first:
            part = o_ref[rows(rb), :] + part
        if last:
            part = x_ref[rows(rb), :] + _rms(part, gpost_ref[...])
        o_ref[rows(rb), :] = part

    _pipelined(x_ref.shape[0] // row_block, gate_up, down)


def _ffn_kernel(*refs, row_block):
    f = pl.program_id(1)
    last_f = pl.num_programs(1) - 1
    for first, last, cond in ((True, False, f == 0),
                              (False, False, (f > 0) & (f < last_f)),
                              (False, True, f == last_f)):
        pl.when(cond)(functools.partial(_ffn_step, *refs, first=first, last=last,
                                        row_block=row_block))


def _ffn(x, g_pre, w_gate, w_up, w_down, g_post, *, layer, tm, tf):
    n, d = x.shape
    f = w_gate.shape[2]
    assert f // tf >= 2, "first and last d_ff chunk are separate code paths"
    return pl.pallas_call(
        functools.partial(_ffn_kernel, row_block=min(FFN_ROW_BLOCK, tm)),
        name="ffn",
        grid=(n // tm, f // tf),
        in_specs=[
            pl.BlockSpec((tm, d), lambda i, j: (i, 0)),
            pl.BlockSpec((1, d), lambda i, j: (0, 0)),
            pl.BlockSpec((None, d, tf), lambda i, j: (layer, 0, j)),
            pl.BlockSpec((None, d, tf), lambda i, j: (layer, 0, j)),
            pl.BlockSpec((None, tf, d), lambda i, j: (layer, j, 0)),
            pl.BlockSpec((1, d), lambda i, j: (0, 0)),
        ],
        out_specs=pl.BlockSpec((tm, d), lambda i, j: (i, 0)),
        out_shape=jax.ShapeDtypeStruct((n, d), F32),
        scratch_shapes=[pltpu.VMEM((tm, d), BF16)],
        compiler_params=_params(2),
    )(x, g_pre, w_gate, w_up, w_down, g_post)


def _conv_step(xm_ref, xp_ref, xn_ref, gpre_ref, wb_ref, wc_ref, wh_ref, wdw_ref, wo_ref,
               gpost_ref, o_ref, hn_ref, *, first, last, row_block, tiles_per_seq):
    tm = xm_ref.shape[0]
    nb = tm // row_block
    w = wdw_ref[...]

    def ext_rows(rb):
        lo = HALO + rb * row_block - (HALO if rb == 0 else 0)
        hi = HALO + (rb + 1) * row_block + (HALO if rb == nb - 1 else 0)
        return lo, hi

    def main_rows(rb):
        return slice(rb * row_block, (rb + 1) * row_block)

    def projections(rb):
        lo, hi = ext_rows(rb)
        if first:
            g = gpre_ref[...]
            hn_ref[HALO + rb * row_block:HALO + (rb + 1) * row_block, :] = (
                _rms(xm_ref[main_rows(rb), :], g).astype(BF16))
            t = pl.program_id(0) % tiles_per_seq
            if rb == 0:
                keep = (t != 0).astype(F32)
                hn_ref[0:HALO, :] = (_rms(xp_ref[...], g) * keep).astype(BF16)
            if rb == nb - 1:
                keep = (t != tiles_per_seq - 1).astype(F32)
                hn_ref[HALO + tm:, :] = (_rms(xn_ref[...], g) * keep).astype(BF16)
        hn = hn_ref[lo:hi, :]
        u = (jnp.dot(hn, wc_ref[...], preferred_element_type=F32)
             * jnp.dot(hn, wh_ref[...], preferred_element_type=F32))
        b = jnp.dot(hn_ref[HALO + rb * row_block:HALO + (rb + 1) * row_block, :], wb_ref[...],
                    preferred_element_type=F32)
        return u, b

    def mix_and_project(rb, u_blocks, b):
        u = u_blocks[rb]
        n_ext = u.shape[0]
        off = HALO if rb == 0 else 0
        u_prev = pltpu.roll(u, 1, 0)[off:off + row_block]
        u_next = pltpu.roll(u, n_ext - 1, 0)[off:off + row_block]
        row = lax.broadcasted_iota(jnp.int32, (row_block, 1), 0)
        if rb > 0:
            u_prev = jnp.where(row == 0, u_blocks[rb - 1][-1:], u_prev)
        if rb < nb - 1:
            u_next = jnp.where(row == row_block - 1, u_blocks[rb + 1][0:1], u_next)
        v = w[0:1] * u_prev + w[1:2] * u[off:off + row_block] + w[2:3] * u_next
        y = (b * v).astype(BF16)
        part = jnp.dot(y, wo_ref[...], preferred_element_type=F32)
        if not first:
            part = o_ref[main_rows(rb), :] + part
        if last:
            part = xm_ref[main_rows(rb), :] + _rms(part, gpost_ref[...])
        o_ref[main_rows(rb), :] = part

    u_blocks, b_blocks = {}, {}
    u_blocks[0], b_blocks[0] = projections(0)
    for rb in range(nb):
        if rb + 1 < nb:
            u_blocks[rb + 1], b_blocks[rb + 1] = projections(rb + 1)
        mix_and_project(rb, u_blocks, b_blocks[rb])


def _conv_kernel(*refs, row_block, tiles_per_seq):
    j = pl.program_id(1)
    last_j = pl.num_programs(1) - 1
    for first, last, cond in ((True, False, j == 0),
                              (False, False, (j > 0) & (j < last_j)),
                              (False, True, j == last_j)):
        pl.when(cond)(functools.partial(_conv_step, *refs, first=first, last=last,
                                        row_block=row_block, tiles_per_seq=tiles_per_seq))


def _conv_mixer(x, g_pre, w_in, w_dw, w_out, g_post, *, layer, seq, tm, tc):
    n, d = x.shape
    nc = d // tc
    assert nc >= 2, "first and last channel chunk are separate code paths"
    hb = tm // HALO
    n_hb = n // HALO
    kern = functools.partial(_conv_kernel, row_block=min(CONV_ROW_BLOCK, tm),
                             tiles_per_seq=seq // tm)
    return pl.pallas_call(
        kern,
        name="conv_mixer",
        grid=(n // tm, nc),
        in_specs=[
            pl.BlockSpec((tm, d), lambda i, j: (i, 0)),
            pl.BlockSpec((HALO, d), lambda i, j: (jnp.maximum(i * hb - 1, 0), 0)),
            pl.BlockSpec((HALO, d), lambda i, j: (jnp.minimum((i + 1) * hb, n_hb - 1), 0)),
            pl.BlockSpec((1, d), lambda i, j: (0, 0)),
            pl.BlockSpec((None, d, tc), lambda i, j: (layer, 0, j)),
            pl.BlockSpec((None, d, tc), lambda i, j: (layer, 0, nc + j)),
            pl.BlockSpec((None, d, tc), lambda i, j: (layer, 0, 2 * nc + j)),
            pl.BlockSpec((None, CONV_WIDTH, tc), lambda i, j: (layer, 0, j)),
            pl.BlockSpec((None, tc, d), lambda i, j: (layer, j, 0)),
            pl.BlockSpec((1, d), lambda i, j: (0, 0)),
        ],
        out_specs=pl.BlockSpec((tm, d), lambda i, j: (i, 0)),
        out_shape=jax.ShapeDtypeStruct((n, d), F32),
        scratch_shapes=[pltpu.VMEM((tm + 2 * HALO, d), BF16)],
        compiler_params=_params(2),
    )(x, x, x, g_pre, w_in, w_in, w_in, w_dw, w_out, g_post)


def _qkv_kernel(x_ref, gpre_ref, w_ref, cos_ref, sin_ref, q_ref, k_ref, v_ref):
    dq = q_ref.shape[1]
    dk = k_ref.shape[1]
    hn = _rms(x_ref[...], gpre_ref[...]).astype(BF16)
    qkv = jnp.dot(hn, w_ref[...], preferred_element_type=F32)
    cos = cos_ref[...]
    sin = sin_ref[...]

    def rope(xh):
        return xh * cos + pltpu.roll(xh, HEAD_DIM // 2, 1) * sin

    for hd in range(dq // HEAD_DIM):
        sl = slice(hd * HEAD_DIM, (hd + 1) * HEAD_DIM)
        q_ref[:, sl] = rope(qkv[:, sl]).astype(BF16)
    for hd in range(dk // HEAD_DIM):
        sl = slice(hd * HEAD_DIM, (hd + 1) * HEAD_DIM)
        k_ref[:, sl] = rope(qkv[:, dq + hd * HEAD_DIM:dq + (hd + 1) * HEAD_DIM]).astype(BF16)
    v_ref[...] = qkv[:, dq + dk:].astype(BF16)


def _qkv_rope(x, g_pre, w_qkv, cos, sin, *, layer, seq, tm):
    n, d = x.shape
    dqkv = w_qkv.shape[2]
    dk = N_KV_HEADS * HEAD_DIM
    dq = dqkv - 2 * dk
    tps = seq // tm
    return pl.pallas_call(
        _qkv_kernel,
        name="qkv_rope",
        grid=(n // tm,),
        in_specs=[
            pl.BlockSpec((tm, d), lambda i: (i, 0)),
            pl.BlockSpec((1, d), lambda i: (0, 0)),
            pl.BlockSpec((None, d, dqkv), lambda i: (layer, 0, 0), pipeline_mode=pl.Buffered(1)),
            pl.BlockSpec((tm, HEAD_DIM), lambda i: (i % tps, 0)),
            pl.BlockSpec((tm, HEAD_DIM), lambda i: (i % tps, 0)),
        ],
        out_specs=[
            pl.BlockSpec((tm, dq), lambda i: (i, 0)),
            pl.BlockSpec((tm, dk), lambda i: (i, 0)),
            pl.BlockSpec((tm, dk), lambda i: (i, 0)),
        ],
        out_shape=[
            jax.ShapeDtypeStruct((n, dq), BF16),
            jax.ShapeDtypeStruct((n, dk), BF16),
            jax.ShapeDtypeStruct((n, dk), BF16),
        ],
        compiler_params=_params(1),
    )(x, g_pre, w_qkv, cos, sin)


def _attn_kernel(sink_ref, x_ref, q_ref, k_ref, v_ref, wo_ref, gpost_ref, o_ref, att_ref, *, seq):
    qi = pl.program_id(1)
    tq = q_ref.shape[0]
    n_heads = q_ref.shape[1] // HEAD_DIM
    group = n_heads // N_KV_HEADS
    exp_coef = HEAD_DIM ** -0.5 * LOG2E
    inv_scale = HEAD_DIM ** 0.5
    row_minus_col = (lax.broadcasted_iota(jnp.int32, (BLOCK, BAND), 0)
                     - lax.broadcasted_iota(jnp.int32, (BLOCK, BAND), 1))
    ones_cols = jnp.ones((BAND, HEAD_DIM), BF16)
    sink_row = ((lax.broadcasted_iota(jnp.int32, (BLOCK, 2 * HEAD_DIM), 0) == 0)
                & (lax.broadcasted_iota(jnp.int32, (BLOCK, 2 * HEAD_DIM), 1) >= HEAD_DIM))
    sink_rows = jnp.where(sink_row, 1.0, 0.0).astype(BF16)

    def window(qb):
        qpos0 = qi * tq + qb * BLOCK
        win0 = pl.multiple_of(jnp.clip(qpos0 - BLOCK, 0, seq - BAND), BLOCK)
        diff = row_minus_col + (qpos0 - win0)
        return win0, (diff <= WINDOW) & (diff >= -WINDOW)

    def scores(qb, kv, win0):
        rows = slice(qb * BLOCK, (qb + 1) * BLOCK)
        kw = k_ref[pl.ds(win0, BAND), kv * HEAD_DIM:(kv + 1) * HEAD_DIM]
        qg = jnp.concatenate(
            [q_ref[rows, (kv * group + g) * HEAD_DIM:(kv * group + g + 1) * HEAD_DIM]
             for g in range(group)], axis=0)
        return lax.dot_general(qg, kw, (((1,), (1,)), ((), ())), preferred_element_type=F32)

    def finish(qb, kv, s, win0, valid):
        rows = slice(qb * BLOCK, (qb + 1) * BLOCK)
        s = jnp.where(valid[None], s.reshape(group, BLOCK, BAND), NEG_INF)
        sink = jnp.concatenate(
            [jnp.full((1, BLOCK, BLOCK), sink_ref[kv * group + g] * inv_scale, F32)
             for g in range(group)], axis=0)
        s = jnp.concatenate([s, sink], axis=-1)
        m = jnp.max(s, axis=-1, keepdims=True)
        p = jnp.exp2((s - m) * exp_coef).astype(BF16).reshape(group * BLOCK, BAND + BLOCK)
        vw = v_ref[pl.ds(win0, BAND), kv * HEAD_DIM:(kv + 1) * HEAD_DIM]
        v_ext = jnp.concatenate([jnp.concatenate([vw, ones_cols], axis=1), sink_rows], axis=0)
        r = jnp.dot(p, v_ext, preferred_element_type=F32)
        og = r[:, :HEAD_DIM] / r[:, HEAD_DIM:]
        for g in range(group):
            hd = kv * group + g
            att_ref[rows, hd * HEAD_DIM:(hd + 1) * HEAD_DIM] = og[g * BLOCK:(g + 1) * BLOCK].astype(BF16)

    steps = [(qb, kv) for qb in range(tq // BLOCK) for kv in range(N_KV_HEADS)]
    win = {qb: window(qb) for qb in range(tq // BLOCK)}
    s_cur = scores(*steps[0], win[steps[0][0]][0])
    for idx, (qb, kv) in enumerate(steps):
        if idx + 1 < len(steps):
            nqb, nkv = steps[idx + 1]
            s_next = scores(nqb, nkv, win[nqb][0])
        finish(qb, kv, s_cur, *win[qb])
        s_cur = s_next

    row_block = min(ATTN_ROW_BLOCK, tq)

    def project(rb):
        return jnp.dot(att_ref[rb * row_block:(rb + 1) * row_block, :], wo_ref[...],
                       preferred_element_type=F32)

    def finish_rows(rb, y):
        rows = slice(rb * row_block, (rb + 1) * row_block)
        o_ref[rows, :] = x_ref[rows, :] + _rms(y, gpost_ref[...])

    _pipelined(tq // row_block, project, finish_rows)


def _attention(x, q, k, v, w_o, sink, g_post, *, layer, seq, tq):
    n, d = x.shape
    dq = q.shape[1]
    dk = k.shape[1]
    tps = seq // tq
    kern = functools.partial(_attn_kernel, seq=seq)
    grid_spec = pltpu.PrefetchScalarGridSpec(
        num_scalar_prefetch=1,
        grid=(n // seq, tps),
        in_specs=[
            pl.BlockSpec((tq, d), lambda b, i, s: (b * tps + i, 0)),
            pl.BlockSpec((tq, dq), lambda b, i, s: (b * tps + i, 0)),
            pl.BlockSpec((seq, dk), lambda b, i, s: (b, 0)),
            pl.BlockSpec((seq, dk), lambda b, i, s: (b, 0)),
            pl.BlockSpec((None, dq, d), lambda b, i, s: (layer, 0, 0), pipeline_mode=pl.Buffered(1)),
            pl.BlockSpec((1, d), lambda b, i, s: (0, 0)),
        ],
        out_specs=pl.BlockSpec((tq, d), lambda b, i, s: (b * tps + i, 0)),
        scratch_shapes=[pltpu.VMEM((tq, dq), BF16)],
    )
    return pl.pallas_call(
        kern,
        name="attention",
        grid_spec=grid_spec,
        out_shape=jax.ShapeDtypeStruct((n, d), F32),
        compiler_params=_params(2),
    )(sink, x, q, k, v, w_o, g_post)


def _rope_tables(seq):
    inv_freq = 1.0 / (ROPE_THETA ** (jnp.arange(0, HEAD_DIM, 2, dtype=F32) / HEAD_DIM))
    ang = jnp.arange(seq, dtype=F32)[:, None] * inv_freq[None, :]
    cos = jnp.cos(ang)
    sin = jnp.sin(ang)
    return jnp.concatenate([cos, cos], axis=-1), jnp.concatenate([-sin, sin], axis=-1)


def _trunk(x3, w, *, tm_ffn, tf, tm_conv, tc, tm_qkv, tq):
    b, seq, d = x3.shape
    x = x3.reshape(b * seq, d)
    depth = w["ffn_w_gate"].shape[0]
    cos, sin = _rope_tables(seq)
    for i in range(depth):
        j = i // 2
        g_pre = w["g_mix_pre"][i][None]
        g_post = w["g_mix_post"][i][None]
        if i % 2 == 0:
            x = _conv_mixer(x, g_pre, w["conv_w_in"], w["conv_w_dw"], w["conv_w_out"], g_post,
                            layer=j, seq=seq, tm=tm_conv, tc=tc)
        else:
            q, k, v = _qkv_rope(x, g_pre, w["attn_w_qkv"], cos, sin, layer=j, seq=seq, tm=tm_qkv)
            x = _attention(x, q, k, v, w["attn_w_o"], w["attn_sink"][j], g_post,
                           layer=j, seq=seq, tq=tq)
        x = _ffn(x, w["g_ffn_pre"][i][None], w["ffn_w_gate"], w["ffn_w_up"], w["ffn_w_down"],
                 w["g_ffn_post"][i][None], layer=i, tm=tm_ffn, tf=tf)
    return x.reshape(b, seq, d)


def kernel(x_prompt, x_sample, conv_w_in, conv_w_dw, conv_w_out, attn_w_qkv, attn_w_o, attn_sink,
           ffn_w_gate, ffn_w_up, ffn_w_down, g_mix_pre, g_mix_post, g_ffn_pre, g_ffn_post):
    w = dict(
        conv_w_in=conv_w_in.astype(BF16), conv_w_dw=conv_w_dw, conv_w_out=conv_w_out.astype(BF16),
        attn_w_qkv=attn_w_qkv.astype(BF16), attn_w_o=attn_w_o.astype(BF16), attn_sink=attn_sink,
        ffn_w_gate=ffn_w_gate.astype(BF16), ffn_w_up=ffn_w_up.astype(BF16),
        ffn_w_down=ffn_w_down.astype(BF16),
        g_mix_pre=g_mix_pre, g_mix_post=g_mix_post, g_ffn_pre=g_ffn_pre, g_ffn_post=g_ffn_post,
    )
    tiles = dict(tm_ffn=1024, tf=512, tm_conv=1024, tc=512, tm_qkv=512, tq=512)
    return _trunk(x_prompt, w, **tiles), _trunk(x_sample, w, **tiles)
```

```python
import functools

import jax
import jax.numpy as jnp
from jax import lax
from jax.experimental import pallas as pl
from jax.experimental.pallas import tpu as pltpu

EPS = 1e-6
NEG_INF = -1e30
LOG2E = 1.4426950408889634
ROPE_THETA = 10000.0
HEAD_DIM = 128
N_KV_HEADS = 4
WINDOW = 128
BLOCK = 128
BAND = 3 * BLOCK
CONV_WIDTH = 3
HALO = 16

BF16 = jnp.bfloat16
F32 = jnp.float32

VMEM_LIMIT_BYTES = 62 * 1024 * 1024


def _rms(x, g):
    r = lax.rsqrt(jnp.mean(x * x, axis=-1, keepdims=True) + EPS)
    return (x * r) * g


ROW_CHUNK = 128
FFN_ROW_BLOCK = 256
CONV_ROW_BLOCK = 256
ATTN_ROW_BLOCK = 256


def _for_row_chunks(n_rows, body):
    chunk = min(ROW_CHUNK, n_rows)

    def step(c, carry):
        body(pl.multiple_of(c * chunk, chunk), chunk)
        return carry

    lax.fori_loop(0, n_rows // chunk, step, 0, unroll=2)


def _prenorm_rows(dst_ref, dst_row0, src_ref, g_ref, n_rows):
    def body(r0, chunk):
        out = _rms(src_ref[pl.ds(r0, chunk), :], g_ref[...]).astype(dst_ref.dtype)
        dst_ref[pl.ds(pl.multiple_of(dst_row0 + r0, HALO), chunk), :] = out

    _for_row_chunks(n_rows, body)


def _postnorm_residual_rows(o_ref, x_ref, g_ref):
    def body(r0, chunk):
        rows = pl.ds(r0, chunk)
        o_ref[rows, :] = x_ref[rows, :] + _rms(o_ref[rows, :], g_ref[...])

    _for_row_chunks(o_ref.shape[0], body)


def _params(n_grid_dims):
    sem = ("parallel",) + ("arbitrary",) * (n_grid_dims - 1)
    return pltpu.CompilerParams(dimension_semantics=sem, vmem_limit_bytes=VMEM_LIMIT_BYTES)


def _pipelined(n_blocks, stage_a, stage_b):
    cur = stage_a(0)
    for i in range(n_blocks):
        nxt = stage_a(i + 1) if i + 1 < n_blocks else None
        stage_b(i, cur)
        cur = nxt


def _ffn_step(x_ref, gpre_ref, wg_ref, wu_ref, wd_ref, gpost_ref, o_ref, hn_ref, *, first, last,
              row_block):
    def rows(rb):
        return slice(rb * row_block, (rb + 1) * row_block)

    def gate_up(rb):
        if first:
            hn_ref[rows(rb), :] = _rms(x_ref[rows(rb), :], gpre_ref[...]).astype(BF16)
        hn = hn_ref[rows(rb), :]
        return (jnp.dot(hn, wg_ref[...], preferred_element_type=F32),
                jnp.dot(hn, wu_ref[...], preferred_element_type=F32))

    def down(rb, gate_and_up):
        gate, up = gate_and_up
        act = (jax.nn.silu(gate) * up).astype(BF16)
        part = jnp.dot(act, wd_ref[...], preferred_element_type=F32)
        if not first:
            part = o_ref[rows(rb), :] + part
        if last:
            part = x_ref[rows(rb), :] + _rms(part, gpost_ref[...])
        o_ref[rows(rb), :] = part

    _pipelined(x_ref.shape[0] // row_block, gate_up, down)


def _ffn_kernel(*refs, row_block):
    f = pl.program_id(1)
    last_f = pl.num_programs(1) - 1
    for first, last, cond in ((True, False, f == 0),
                              (False, False, (f > 0) & (f < last_f)),
                              (False, True, f == last_f)):
        pl.when(cond)(functools.partial(_ffn_step, *refs, first=first, last=last,
                                        row_block=row_block))


def _ffn(x, g_pre, w_gate, w_up, w_down, g_post, *, layer, tm, tf):
    n, d = x.shape
    f = w_gate.shape[2]
    assert f // tf >= 2, "first and last d_ff chunk are separate code paths"
    return pl.pallas_call(
        functools.partial(_ffn_kernel, row_block=min(FFN_ROW_BLOCK, tm)),
        name="ffn",
        grid=(n // tm, f // tf),
        in_specs=[
            pl.BlockSpec((tm, d), lambda i, j: (i, 0)),
            pl.BlockSpec((1, d), lambda i, j: (0, 0)),
            pl.BlockSpec((None, d, tf), lambda i, j: (layer, 0, j)),
            pl.BlockSpec((None, d, tf), lambda i, j: (layer, 0, j)),
            pl.BlockSpec((None, tf, d), lambda i, j: (layer, j, 0)),
            pl.BlockSpec((1, d), lambda i, j: (0, 0)),
        ],
        out_specs=pl.BlockSpec((tm, d), lambda i, j: (i, 0)),
        out_shape=jax.ShapeDtypeStruct((n, d), F32),
        scratch_shapes=[pltpu.VMEM((tm, d), BF16)],
        compiler_params=_params(2),
    )(x, g_pre, w_gate, w_up, w_down, g_post)


def _conv_step(xm_ref, xp_ref, xn_ref, gpre_ref, wb_ref, wc_ref, wh_ref, wdw_ref, wo_ref,
               gpost_ref, o_ref, hn_ref, *, first, last, row_block, tiles_per_seq):
    tm = xm_ref.shape[0]
    nb = tm // row_block
    w = wdw_ref[...]

    def ext_rows(rb):
        lo = HALO + rb * row_block - (HALO if rb == 0 else 0)
        hi = HALO + (rb + 1) * row_block + (HALO if rb == nb - 1 else 0)
        return lo, hi

    def main_rows(rb):
        return slice(rb * row_block, (rb + 1) * row_block)

    def projections(rb):
        lo, hi = ext_rows(rb)
        if first:
            g = gpre_ref[...]
            hn_ref[HALO + rb * row_block:HALO + (rb + 1) * row_block, :] = (
                _rms(xm_ref[main_rows(rb), :], g).astype(BF16))
            t = pl.program_id(0) % tiles_per_seq
            if rb == 0:
                keep = (t != 0).astype(F32)
                hn_ref[0:HALO, :] = (_rms(xp_ref[...], g) * keep).astype(BF16)
            if rb == nb - 1:
                keep = (t != tiles_per_seq - 1).astype(F32)
                hn_ref[HALO + tm:, :] = (_rms(xn_ref[...], g) * keep).astype(BF16)
        hn = hn_ref[lo:hi, :]
        u = (jnp.dot(hn, wc_ref[...], preferred_element_type=F32)
             * jnp.dot(hn, wh_ref[...], preferred_element_type=F32))
        b = jnp.dot(hn_ref[HALO + rb * row_block:HALO + (rb + 1) * row_block, :], wb_ref[...],
                    preferred_element_type=F32)
        return u, b

    def mix_and_project(rb, u_blocks, b):
        u = u_blocks[rb]
        n_ext = u.shape[0]
        off = HALO if rb == 0 else 0
        u_prev = pltpu.roll(u, 1, 0)[off:off + row_block]
        u_next = pltpu.roll(u, n_ext - 1, 0)[off:off + row_block]
        row = lax.broadcasted_iota(jnp.int32, (row_block, 1), 0)
        if rb > 0:
            u_prev = jnp.where(row == 0, u_blocks[rb - 1][-1:], u_prev)
        if rb < nb - 1:
            u_next = jnp.where(row == row_block - 1, u_blocks[rb + 1][0:1], u_next)
        v = w[0:1] * u_prev + w[1:2] * u[off:off + row_block] + w[2:3] * u_next
        y = (b * v).astype(BF16)
        part = jnp.dot(y, wo_ref[...], preferred_element_type=F32)
        if not first:
            part = o_ref[main_rows(rb), :] + part
        if last:
            part = xm_ref[main_rows(rb), :] + _rms(part, gpost_ref[...])
        o_ref[main_rows(rb), :] = part

    u_blocks, b_blocks = {}, {}
    u_blocks[0], b_blocks[0] = projections(0)
    for rb in range(nb):
        if rb + 1 < nb:
            u_blocks[rb + 1], b_blocks[rb + 1] = projections(rb + 1)
        mix_and_project(rb, u_blocks, b_blocks[rb])


def _conv_kernel(*refs, row_block, tiles_per_seq):
    j = pl.program_id(1)
    last_j = pl.num_programs(1) - 1
    for first, last, cond in ((True, False, j == 0),
                              (False, False, (j > 0) & (j < last_j)),
                              (False, True, j == last_j)):
        pl.when(cond)(functools.partial(_conv_step, *refs, first=first, last=last,
                                        row_block=row_block, tiles_per_seq=tiles_per_seq))


def _conv_mixer(x, g_pre, w_in, w_dw, w_out, g_post, *, layer, seq, tm, tc):
    n, d = x.shape
    nc = d // tc
    assert nc >= 2, "first and last channel chunk are separate code paths"
    hb = tm // HALO
    n_hb = n // HALO
    kern = functools.partial(_conv_kernel, row_block=min(CONV_ROW_BLOCK, tm),
                             tiles_per_seq=seq // tm)
    return pl.pallas_call(
        kern,
        name="conv_mixer",
        grid=(n // tm, nc),
        in_specs=[
            pl.BlockSpec((tm, d), lambda i, j: (i, 0)),
            pl.BlockSpec((HALO, d), lambda i, j: (jnp.maximum(i * hb - 1, 0), 0)),
            pl.BlockSpec((HALO, d), lambda i, j: (jnp.minimum((i + 1) * hb, n_hb - 1), 0)),
            pl.BlockSpec((1, d), lambda i, j: (0, 0)),
            pl.BlockSpec((None, d, tc), lambda i, j: (layer, 0, j)),
            pl.BlockSpec((None, d, tc), lambda i, j: (layer, 0, nc + j)),
            pl.BlockSpec((None, d, tc), lambda i, j: (layer, 0, 2 * nc + j)),
            pl.BlockSpec((None, CONV_WIDTH, tc), lambda i, j: (layer, 0, j)),
            pl.BlockSpec((None, tc, d), lambda i, j: (layer, j, 0)),
            pl.BlockSpec((1, d), lambda i, j: (0, 0)),
        ],
        out_specs=pl.BlockSpec((tm, d), lambda i, j: (i, 0)),
        out_shape=jax.ShapeDtypeStruct((n, d), F32),
        scratch_shapes=[pltpu.VMEM((tm + 2 * HALO, d), BF16)],
        compiler_params=_params(2),
    )(x, x, x, g_pre, w_in, w_in, w_in, w_dw, w_out, g_post)


def _qkv_kernel(x_ref, gpre_ref, w_ref, cos_ref, sin_ref, q_ref, k_ref, v_ref):
    dq = q_ref.shape[1]
    dk = k_ref.shape[1]
    hn = _rms(x_ref[...], gpre_ref[...]).astype(BF16)
    qkv = jnp.dot(hn, w_ref[...], preferred_element_type=F32)
    cos = cos_ref[...]
    sin = sin_ref[...]

    def rope(xh):
        return xh * cos + pltpu.roll(xh, HEAD_DIM // 2, 1) * sin

    for hd in range(dq // HEAD_DIM):
        sl = slice(hd * HEAD_DIM, (hd + 1) * HEAD_DIM)
        q_ref[:, sl] = rope(qkv[:, sl]).astype(BF16)
    for hd in range(dk // HEAD_DIM):
        sl = slice(hd * HEAD_DIM, (hd + 1) * HEAD_DIM)
        k_ref[:, sl] = rope(qkv[:, dq + hd * HEAD_DIM:dq + (hd + 1) * HEAD_DIM]).astype(BF16)
    v_ref[...] = qkv[:, dq + dk:].astype(BF16)


def _qkv_rope(x, g_pre, w_qkv, cos, sin, *, layer, seq, tm):
    n, d = x.shape
    dqkv = w_qkv.shape[2]
    dk = N_KV_HEADS * HEAD_DIM
    dq = dqkv - 2 * dk
    tps = seq // tm
    return pl.pallas_call(
        _qkv_kernel,
        name="qkv_rope",
        grid=(n // tm,),
        in_specs=[
            pl.BlockSpec((tm, d), lambda i: (i, 0)),
            pl.BlockSpec((1, d), lambda i: (0, 0)),
            pl.BlockSpec((None, d, dqkv), lambda i: (layer, 0, 0), pipeline_mode=pl.Buffered(1)),
            pl.BlockSpec((tm, HEAD_DIM), lambda i: (i % tps, 0)),
            pl.BlockSpec((tm, HEAD_DIM), lambda i: (i % tps, 0)),
        ],
        out_specs=[
            pl.BlockSpec((tm, dq), lambda i: (i, 0)),
            pl.BlockSpec((tm, dk), lambda i: (i, 0)),
            pl.BlockSpec((tm, dk), lambda i: (i, 0)),
        ],
        out_shape=[
            jax.ShapeDtypeStruct((n, dq), BF16),
            jax.ShapeDtypeStruct((n, dk), BF16),
            jax.ShapeDtypeStruct((n, dk), BF16),
        ],
        compiler_params=_params(1),
    )(x, g_pre, w_qkv, cos, sin)


def _attn_kernel(sink_ref, x_ref, q_ref, k_ref, v_ref, wo_ref, gpost_ref, o_ref, att_ref, *, seq):
    qi = pl.program_id(1)
    tq = q_ref.shape[0]
    n_heads = q_ref.shape[1] // HEAD_DIM
    group = n_heads // N_KV_HEADS
    exp_coef = HEAD_DIM ** -0.5 * LOG2E
    inv_scale = HEAD_DIM ** 0.5
    qrow_minus_krow = (lax.broadcasted_iota(jnp.int32, (BAND, BLOCK), 1)
                       - lax.broadcasted_iota(jnp.int32, (BAND, BLOCK), 0))
    one_hot_rows = (lax.broadcasted_iota(jnp.int32, (BLOCK, BLOCK), 0)
                    == lax.broadcasted_iota(jnp.int32, (BLOCK, BLOCK), 1))
    one_hot_rows = jnp.where(one_hot_rows, 1.0, 0.0).astype(BF16)
    ones_cols = jnp.ones((BAND, HEAD_DIM), BF16)
    sink_row = ((lax.broadcasted_iota(jnp.int32, (BLOCK, 2 * HEAD_DIM), 0) == 0)
                & (lax.broadcasted_iota(jnp.int32, (BLOCK, 2 * HEAD_DIM), 1) >= HEAD_DIM))
    sink_rows = jnp.where(sink_row, 1.0, 0.0).astype(BF16)

    def window(qb):
        qpos0 = qi * tq + qb * BLOCK
        win0 = pl.multiple_of(jnp.clip(qpos0 - BLOCK, 0, seq - BAND), BLOCK)
        diff = qrow_minus_krow + (qpos0 - win0)
        valid = (diff <= WINDOW) & (diff >= -WINDOW)
        return win0, jnp.where(valid, 0.0, NEG_INF).astype(BF16)

    def scores(qb, kv, win0, mask):
        rows = slice(qb * BLOCK, (qb + 1) * BLOCK)
        kw = k_ref[pl.ds(win0, BAND), kv * HEAD_DIM:(kv + 1) * HEAD_DIM]
        k_ext = jnp.concatenate([kw, mask], axis=1)
        q_ext = jnp.concatenate(
            [jnp.concatenate(
                [q_ref[rows, (kv * group + g) * HEAD_DIM:(kv * group + g + 1) * HEAD_DIM],
                 one_hot_rows], axis=1)
             for g in range(group)], axis=0)
        return lax.dot_general(q_ext, k_ext, (((1,), (1,)), ((), ())), preferred_element_type=F32)

    def probabilities(kv, s):
        s = s.reshape(group, BLOCK, BAND)
        sink = jnp.concatenate(
            [jnp.full((1, BLOCK, BLOCK), sink_ref[kv * group + g] * inv_scale, F32)
             for g in range(group)], axis=0)
        s = jnp.concatenate([s, sink], axis=-1)
        m = jnp.max(s, axis=-1, keepdims=True)
        return jnp.exp2((s - m) * exp_coef).astype(BF16).reshape(group * BLOCK, BAND + BLOCK)

    def weighted_values(qb, kv, p, win0):
        rows = slice(qb * BLOCK, (qb + 1) * BLOCK)
        vw = v_ref[pl.ds(win0, BAND), kv * HEAD_DIM:(kv + 1) * HEAD_DIM]
        v_ext = jnp.concatenate([jnp.concatenate([vw, ones_cols], axis=1), sink_rows], axis=0)
        r = jnp.dot(p, v_ext, preferred_element_type=F32)
        og = r[:, :HEAD_DIM] / r[:, HEAD_DIM:]
        for g in range(group):
            hd = kv * group + g
            att_ref[rows, hd * HEAD_DIM:(hd + 1) * HEAD_DIM] = og[g * BLOCK:(g + 1) * BLOCK].astype(BF16)

    row_block = min(ATTN_ROW_BLOCK, tq)

    def project_rows(rb):
        rows = slice(rb * row_block, (rb + 1) * row_block)
        y = jnp.dot(att_ref[rows, :], wo_ref[...], preferred_element_type=F32)
        o_ref[rows, :] = x_ref[rows, :] + _rms(y, gpost_ref[...])

    steps = [(qb, kv) for qb in range(tq // BLOCK) for kv in range(N_KV_HEADS)]
    win = {qb: window(qb) for qb in range(tq // BLOCK)}
    steps_per_row_block = (row_block // BLOCK) * N_KV_HEADS
    s_cur = scores(*steps[0], *win[steps[0][0]])
    for idx, (qb, kv) in enumerate(steps):
        if idx + 1 < len(steps):
            nqb, nkv = steps[idx + 1]
            s_next = scores(nqb, nkv, *win[nqb])
        weighted_values(qb, kv, probabilities(kv, s_cur), win[qb][0])
        s_cur = s_next
        if (idx + 1) % steps_per_row_block == 0:
            project_rows(idx // steps_per_row_block)


def _attention(x, q, k, v, w_o, sink, g_post, *, layer, seq, tq):
    n, d = x.shape
    dq = q.shape[1]
    dk = k.shape[1]
    tps = seq // tq
    kern = functools.partial(_attn_kernel, seq=seq)
    grid_spec = pltpu.PrefetchScalarGridSpec(
        num_scalar_prefetch=1,
        grid=(n // seq, tps),
        in_specs=[
            pl.BlockSpec((tq, d), lambda b, i, s: (b * tps + i, 0)),
            pl.BlockSpec((tq, dq), lambda b, i, s: (b * tps + i, 0)),
            pl.BlockSpec((seq, dk), lambda b, i, s: (b, 0)),
            pl.BlockSpec((seq, dk), lambda b, i, s: (b, 0)),
            pl.BlockSpec((None, dq, d), lambda b, i, s: (layer, 0, 0), pipeline_mode=pl.Buffered(1)),
            pl.BlockSpec((1, d), lambda b, i, s: (0, 0)),
        ],
        out_specs=pl.BlockSpec((tq, d), lambda b, i, s: (b * tps + i, 0)),
        scratch_shapes=[pltpu.VMEM((tq, dq), BF16)],
    )
    return pl.pallas_call(
        kern,
        name="attention",
        grid_spec=grid_spec,
        out_shape=jax.ShapeDtypeStruct((n, d), F32),
        compiler_params=_params(2),
    )(sink, x, q, k, v, w_o, g_post)


def _rope_tables(seq):
    inv_freq = 1.0 / (ROPE_THETA ** (jnp.arange(0, HEAD_DIM, 2, dtype=F32) / HEAD_DIM))
    ang = jnp.arange(seq, dtype=F32)[:, None] * inv_freq[None, :]
    cos = jnp.cos(ang)
    sin = jnp.sin(ang)
    return jnp.concatenate([cos, cos], axis=-1), jnp.concatenate([-sin, sin], axis=-1)


def _trunk(x3, w, *, tm_ffn, tf, tm_conv, tc, tm_qkv, tq):
    b, seq, d = x3.shape
    x = x3.reshape(b * seq, d)
    depth = w["ffn_w_gate"].shape[0]
    cos, sin = _rope_tables(seq)
    for i in range(depth):
        j = i // 2
        g_pre = w["g_mix_pre"][i][None]
        g_post = w["g_mix_post"][i][None]
        if i % 2 == 0:
            x = _conv_mixer(x, g_pre, w["conv_w_in"], w["conv_w_dw"], w["conv_w_out"], g_post,
                            layer=j, seq=seq, tm=tm_conv, tc=tc)
        else:
            q, k, v = _qkv_rope(x, g_pre, w["attn_w_qkv"], cos, sin, layer=j, seq=seq, tm=tm_qkv)
            x = _attention(x, q, k, v, w["attn_w_o"], w["attn_sink"][j], g_post,
                           layer=j, seq=seq, tq=tq)
        x = _ffn(x, w["g_ffn_pre"][i][None], w["ffn_w_gate"], w["ffn_w_up"], w["ffn_w_down"],
                 w["g_ffn_post"][i][None], layer=i, tm=tm_ffn, tf=tf)
    return x.reshape(b, seq, d)


def kernel(x_prompt, x_sample, conv_w_in, conv_w_dw, conv_w_out, attn_w_qkv, attn_w_o, attn_sink,
           ffn_w_gate, ffn_w_up, ffn_w_down, g_mix_pre, g_mix_post, g_ffn_pre, g_ffn_post):
    w = dict(
        conv_w_in=conv_w_in.astype(BF16), conv_w_dw=conv_w_dw, conv_w_out=conv_w_out.astype(BF16),
        attn_w_qkv=attn_w_qkv.astype(BF16), attn_w_o=attn_w_o.astype(BF16), attn_sink=attn_sink,
        ffn_w_gate=ffn_w_gate.astype(BF16), ffn_w_up=ffn_w_up.astype(BF16),
        ffn_w_down=ffn_w_down.astype(BF16),
        g_mix_pre=g_mix_pre, g_mix_post=g_mix_post, g_ffn_pre=g_ffn_pre, g_ffn_post=g_ffn_post,
    )
    tiles = dict(tm_ffn=1024, tf=512, tm_conv=1024, tc=512, tm_qkv=512, tq=512)
    return _trunk(x_prompt, w, **tiles), _trunk(x_sample, w, **tiles)
```

```python
import functools

import jax
import jax.numpy as jnp
from jax import lax
from jax.experimental import pallas as pl
from jax.experimental.pallas import tpu as pltpu

EPS = 1e-6
NEG_INF = -1e30
LOG2E = 1.4426950408889634
ROPE_THETA = 10000.0
HEAD_DIM = 128
N_KV_HEADS = 4
WINDOW = 128
BLOCK = 128
BAND = 3 * BLOCK
CONV_WIDTH = 3
HALO = 16

BF16 = jnp.bfloat16
F32 = jnp.float32

VMEM_LIMIT_BYTES = 62 * 1024 * 1024


def _rms(x, g):
    r = lax.rsqrt(jnp.mean(x * x, axis=-1, keepdims=True) + EPS)
    return (x * r) * g


ROW_CHUNK = 128
FFN_ROW_BLOCK = 256
CONV_ROW_BLOCK = 256
ATTN_ROW_BLOCK = 256
CAST_LANES = 1024
FFN_WEIGHTS = ("ffn_w_gate", "ffn_w_up", "ffn_w_down")


def _for_row_chunks(n_rows, body):
    chunk = min(ROW_CHUNK, n_rows)

    def step(c, carry):
        body(pl.multiple_of(c * chunk, chunk), chunk)
        return carry

    lax.fori_loop(0, n_rows // chunk, step, 0, unroll=2)


def _prenorm_rows(dst_ref, dst_row0, src_ref, g_ref, n_rows):
    def body(r0, chunk):
        out = _rms(src_ref[pl.ds(r0, chunk), :], g_ref[...]).astype(dst_ref.dtype)
        dst_ref[pl.ds(pl.multiple_of(dst_row0 + r0, HALO), chunk), :] = out

    _for_row_chunks(n_rows, body)


def _postnorm_residual_rows(o_ref, x_ref, g_ref):
    def body(r0, chunk):
        rows = pl.ds(r0, chunk)
        o_ref[rows, :] = x_ref[rows, :] + _rms(o_ref[rows, :], g_ref[...])

    _for_row_chunks(o_ref.shape[0], body)


def _params(n_grid_dims):
    sem = ("parallel",) + ("arbitrary",) * (n_grid_dims - 1)
    return pltpu.CompilerParams(dimension_semantics=sem, vmem_limit_bytes=VMEM_LIMIT_BYTES)


def _pipelined(n_blocks, stage_a, stage_b):
    cur = stage_a(0)
    for i in range(n_blocks):
        nxt = stage_a(i + 1) if i + 1 < n_blocks else None
        stage_b(i, cur)
        cur = nxt


def _ffn_step(x_ref, gpre_ref, wg_ref, wu_ref, wd_ref, gpost_ref, o_ref, hn_ref, *, first, last,
              row_block):
    def rows(rb):
        return slice(rb * row_block, (rb + 1) * row_block)

    def gate_up(rb):
        if first:
            hn_ref[rows(rb), :] = _rms(x_ref[rows(rb), :], gpre_ref[...]).astype(BF16)
        hn = hn_ref[rows(rb), :]
        return (jnp.dot(hn, wg_ref[...], preferred_element_type=F32),
                jnp.dot(hn, wu_ref[...], preferred_element_type=F32))

    def down(rb, gate_and_up):
        gate, up = gate_and_up
        act = (jax.nn.silu(gate) * up).astype(BF16)
        part = jnp.dot(act, wd_ref[...], preferred_element_type=F32)
        if not first:
            part = o_ref[rows(rb), :] + part
        if last:
            part = x_ref[rows(rb), :] + _rms(part, gpost_ref[...])
        o_ref[rows(rb), :] = part

    _pipelined(x_ref.shape[0] // row_block, gate_up, down)


def _ffn_kernel(x_ref, gpre_ref, wg_ref, wu_ref, wd_ref, gpost_ref, *rest, row_block, n_cast):
    cast_src, o_ref, cast_dst, hn_ref = (rest[:n_cast], rest[n_cast],
                                         rest[n_cast + 1:2 * n_cast + 1], rest[-1])
    for src, dst in zip(cast_src, cast_dst):
        dst[...] = src[...].astype(dst.dtype)

    refs = (x_ref, gpre_ref, wg_ref, wu_ref, wd_ref, gpost_ref, o_ref, hn_ref)
    f = pl.program_id(1)
    last_f = pl.num_programs(1) - 1
    for first, last, cond in ((True, False, f == 0),
                              (False, False, (f > 0) & (f < last_f)),
                              (False, True, f == last_f)):
        pl.when(cond)(functools.partial(_ffn_step, *refs, first=first, last=last,
                                        row_block=row_block))


def _ffn(x, g_pre, w_gate, w_up, w_down, g_post, *, tm, tf, cast_next=None):
    n, d = x.shape
    f = w_gate.shape[1]
    nf = f // tf
    assert nf >= 2, "first and last d_ff chunk are separate code paths"
    n_steps = (n // tm) * nf
    cast_in, cast_specs_in, cast_specs_out, cast_shapes = [], [], [], []
    if cast_next is not None:
        stacked, nxt = cast_next
        for w32 in stacked:
            depth = w32.shape[0]
            slabs = w32.reshape(depth, -1, CAST_LANES)
            rows = slabs.shape[1] // n_steps
            assert rows * n_steps == slabs.shape[1] and rows % HALO == 0
            cast_in.append(slabs)
            cast_specs_in.append(
                pl.BlockSpec((None, rows, CAST_LANES), lambda i, j: (nxt, i * nf + j, 0)))
            cast_specs_out.append(pl.BlockSpec((rows, CAST_LANES), lambda i, j: (i * nf + j, 0)))
            cast_shapes.append(jax.ShapeDtypeStruct(slabs.shape[1:], BF16))
    outs = pl.pallas_call(
        functools.partial(_ffn_kernel, row_block=min(FFN_ROW_BLOCK, tm), n_cast=len(cast_in)),
        name="ffn",
        grid=(n // tm, nf),
        in_specs=[
            pl.BlockSpec((tm, d), lambda i, j: (i, 0)),
            pl.BlockSpec((1, d), lambda i, j: (0, 0)),
            pl.BlockSpec((d, tf), lambda i, j: (0, j)),
            pl.BlockSpec((d, tf), lambda i, j: (0, j)),
            pl.BlockSpec((tf, d), lambda i, j: (j, 0)),
            pl.BlockSpec((1, d), lambda i, j: (0, 0)),
        ] + cast_specs_in,
        out_specs=[pl.BlockSpec((tm, d), lambda i, j: (i, 0))] + cast_specs_out,
        out_shape=[jax.ShapeDtypeStruct((n, d), F32)] + cast_shapes,
        scratch_shapes=[pltpu.VMEM((tm, d), BF16)],
        compiler_params=_params(2),
    )(x, g_pre, w_gate, w_up, w_down, g_post, *cast_in)
    if cast_next is None:
        return outs[0], None
    return outs[0], tuple(o.reshape(w32.shape[1:]) for o, w32 in zip(outs[1:], cast_next[0]))


def _conv_step(xm_ref, xp_ref, xn_ref, gpre_ref, wb_ref, wc_ref, wh_ref, wdw_ref, wo_ref,
               gpost_ref, o_ref, hn_ref, *, first, last, row_block, tiles_per_seq):
    tm = xm_ref.shape[0]
    nb = tm // row_block
    w = wdw_ref[...]

    def ext_rows(rb):
        lo = HALO + rb * row_block - (HALO if rb == 0 else 0)
        hi = HALO + (rb + 1) * row_block + (HALO if rb == nb - 1 else 0)
        return lo, hi

    def main_rows(rb):
        return slice(rb * row_block, (rb + 1) * row_block)

    def projections(rb):
        lo, hi = ext_rows(rb)
        if first:
            g = gpre_ref[...]
            hn_ref[HALO + rb * row_block:HALO + (rb + 1) * row_block, :] = (
                _rms(xm_ref[main_rows(rb), :], g).astype(BF16))
            t = pl.program_id(0) % tiles_per_seq
            if rb == 0:
                keep = (t != 0).astype(F32)
                hn_ref[0:HALO, :] = (_rms(xp_ref[...], g) * keep).astype(BF16)
            if rb == nb - 1:
                keep = (t != tiles_per_seq - 1).astype(F32)
                hn_ref[HALO + tm:, :] = (_rms(xn_ref[...], g) * keep).astype(BF16)
        hn = hn_ref[lo:hi, :]
        u = (jnp.dot(hn, wc_ref[...], preferred_element_type=F32)
             * jnp.dot(hn, wh_ref[...], preferred_element_type=F32))
        b = jnp.dot(hn_ref[HALO + rb * row_block:HALO + (rb + 1) * row_block, :], wb_ref[...],
                    preferred_element_type=F32)
        return u, b

    def mix_and_project(rb, u_blocks, b):
        u = u_blocks[rb]
        n_ext = u.shape[0]
        off = HALO if rb == 0 else 0
        u_prev = pltpu.roll(u, 1, 0)[off:off + row_block]
        u_next = pltpu.roll(u, n_ext - 1, 0)[off:off + row_block]
        row = lax.broadcasted_iota(jnp.int32, (row_block, 1), 0)
        if rb > 0:
            u_prev = jnp.where(row == 0, u_blocks[rb - 1][-1:], u_prev)
        if rb < nb - 1:
            u_next = jnp.where(row == row_block - 1, u_blocks[rb + 1][0:1], u_next)
        v = w[0:1] * u_prev + w[1:2] * u[off:off + row_block] + w[2:3] * u_next
        y = (b * v).astype(BF16)
        part = jnp.dot(y, wo_ref[...], preferred_element_type=F32)
        if not first:
            part = o_ref[main_rows(rb), :] + part
        if last:
            part = xm_ref[main_rows(rb), :] + _rms(part, gpost_ref[...])
        o_ref[main_rows(rb), :] = part

    u_blocks, b_blocks = {}, {}
    u_blocks[0], b_blocks[0] = projections(0)
    for rb in range(nb):
        if rb + 1 < nb:
            u_blocks[rb + 1], b_blocks[rb + 1] = projections(rb + 1)
        mix_and_project(rb, u_blocks, b_blocks[rb])


def _conv_kernel(*refs, row_block, tiles_per_seq):
    j = pl.program_id(1)
    last_j = pl.num_programs(1) - 1
    for first, last, cond in ((True, False, j == 0),
                              (False, False, (j > 0) & (j < last_j)),
                              (False, True, j == last_j)):
        pl.when(cond)(functools.partial(_conv_step, *refs, first=first, last=last,
                                        row_block=row_block, tiles_per_seq=tiles_per_seq))


def _conv_mixer(x, g_pre, w_in, w_dw, w_out, g_post, *, layer, seq, tm, tc):
    n, d = x.shape
    nc = d // tc
    assert nc >= 2, "first and last channel chunk are separate code paths"
    hb = tm // HALO
    n_hb = n // HALO
    kern = functools.partial(_conv_kernel, row_block=min(CONV_ROW_BLOCK, tm),
                             tiles_per_seq=seq // tm)
    return pl.pallas_call(
        kern,
        name="conv_mixer",
        grid=(n // tm, nc),
        in_specs=[
            pl.BlockSpec((tm, d), lambda i, j: (i, 0)),
            pl.BlockSpec((HALO, d), lambda i, j: (jnp.maximum(i * hb - 1, 0), 0)),
            pl.BlockSpec((HALO, d), lambda i, j: (jnp.minimum((i + 1) * hb, n_hb - 1), 0)),
            pl.BlockSpec((1, d), lambda i, j: (0, 0)),
            pl.BlockSpec((None, d, tc), lambda i, j: (layer, 0, j)),
            pl.BlockSpec((None, d, tc), lambda i, j: (layer, 0, nc + j)),
            pl.BlockSpec((None, d, tc), lambda i, j: (layer, 0, 2 * nc + j)),
            pl.BlockSpec((None, CONV_WIDTH, tc), lambda i, j: (layer, 0, j)),
            pl.BlockSpec((None, tc, d), lambda i, j: (layer, j, 0)),
            pl.BlockSpec((1, d), lambda i, j: (0, 0)),
        ],
        out_specs=pl.BlockSpec((tm, d), lambda i, j: (i, 0)),
        out_shape=jax.ShapeDtypeStruct((n, d), F32),
        scratch_shapes=[pltpu.VMEM((tm + 2 * HALO, d), BF16)],
        compiler_params=_params(2),
    )(x, x, x, g_pre, w_in, w_in, w_in, w_dw, w_out, g_post)


def _qkv_kernel(x_ref, gpre_ref, w_ref, cos_ref, sin_ref, q_ref, k_ref, v_ref):
    dq = q_ref.shape[1]
    dk = k_ref.shape[1]
    hn = _rms(x_ref[...], gpre_ref[...]).astype(BF16)
    qkv = jnp.dot(hn, w_ref[...], preferred_element_type=F32)
    cos = cos_ref[...]
    sin = sin_ref[...]

    def rope(xh):
        return xh * cos + pltpu.roll(xh, HEAD_DIM // 2, 1) * sin

    for hd in range(dq // HEAD_DIM):
        sl = slice(hd * HEAD_DIM, (hd + 1) * HEAD_DIM)
        q_ref[:, sl] = rope(qkv[:, sl]).astype(BF16)
    for hd in range(dk // HEAD_DIM):
        sl = slice(hd * HEAD_DIM, (hd + 1) * HEAD_DIM)
        k_ref[:, sl] = rope(qkv[:, dq + hd * HEAD_DIM:dq + (hd + 1) * HEAD_DIM]).astype(BF16)
    v_ref[...] = qkv[:, dq + dk:].astype(BF16)


def _qkv_rope(x, g_pre, w_qkv, cos, sin, *, layer, seq, tm):
    n, d = x.shape
    dqkv = w_qkv.shape[2]
    dk = N_KV_HEADS * HEAD_DIM
    dq = dqkv - 2 * dk
    tps = seq // tm
    return pl.pallas_call(
        _qkv_kernel,
        name="qkv_rope",
        grid=(n // tm,),
        in_specs=[
            pl.BlockSpec((tm, d), lambda i: (i, 0)),
            pl.BlockSpec((1, d), lambda i: (0, 0)),
            pl.BlockSpec((None, d, dqkv), lambda i: (layer, 0, 0), pipeline_mode=pl.Buffered(1)),
            pl.BlockSpec((tm, HEAD_DIM), lambda i: (i % tps, 0)),
            pl.BlockSpec((tm, HEAD_DIM), lambda i: (i % tps, 0)),
        ],
        out_specs=[
            pl.BlockSpec((tm, dq), lambda i: (i, 0)),
            pl.BlockSpec((tm, dk), lambda i: (i, 0)),
            pl.BlockSpec((tm, dk), lambda i: (i, 0)),
        ],
        out_shape=[
            jax.ShapeDtypeStruct((n, dq), BF16),
            jax.ShapeDtypeStruct((n, dk), BF16),
            jax.ShapeDtypeStruct((n, dk), BF16),
        ],
        compiler_params=_params(1),
    )(x, g_pre, w_qkv, cos, sin)


def _attn_kernel(sink_ref, x_ref, q_ref, k_ref, v_ref, wo_ref, gpost_ref, o_ref, att_ref, *, seq):
    qi = pl.program_id(1)
    tq = q_ref.shape[0]
    n_heads = q_ref.shape[1] // HEAD_DIM
    group = n_heads // N_KV_HEADS
    exp_coef = HEAD_DIM ** -0.5 * LOG2E
    inv_scale = HEAD_DIM ** 0.5
    qrow_minus_krow = (lax.broadcasted_iota(jnp.int32, (BAND, BLOCK), 1)
                       - lax.broadcasted_iota(jnp.int32, (BAND, BLOCK), 0))
    one_hot_rows = (lax.broadcasted_iota(jnp.int32, (BLOCK, BLOCK), 0)
                    == lax.broadcasted_iota(jnp.int32, (BLOCK, BLOCK), 1))
    one_hot_rows = jnp.where(one_hot_rows, 1.0, 0.0).astype(BF16)
    ones_cols = jnp.ones((BAND, HEAD_DIM), BF16)
    sink_row = ((lax.broadcasted_iota(jnp.int32, (BLOCK, 2 * HEAD_DIM), 0) == 0)
                & (lax.broadcasted_iota(jnp.int32, (BLOCK, 2 * HEAD_DIM), 1) >= HEAD_DIM))
    sink_rows = jnp.where(sink_row, 1.0, 0.0).astype(BF16)

    def window(qb):
        qpos0 = qi * tq + qb * BLOCK
        win0 = pl.multiple_of(jnp.clip(qpos0 - BLOCK, 0, seq - BAND), BLOCK)
        diff = qrow_minus_krow + (qpos0 - win0)
        valid = (diff <= WINDOW) & (diff >= -WINDOW)
        return win0, jnp.where(valid, 0.0, NEG_INF).astype(BF16)

    def scores(qb, kv, win0, mask):
        rows = slice(qb * BLOCK, (qb + 1) * BLOCK)
        kw = k_ref[pl.ds(win0, BAND), kv * HEAD_DIM:(kv + 1) * HEAD_DIM]
        k_ext = jnp.concatenate([kw, mask], axis=1)
        q_ext = jnp.concatenate(
            [jnp.concatenate(
                [q_ref[rows, (kv * group + g) * HEAD_DIM:(kv * group + g + 1) * HEAD_DIM],
                 one_hot_rows], axis=1)
             for g in range(group)], axis=0)
        return lax.dot_general(q_ext, k_ext, (((1,), (1,)), ((), ())), preferred_element_type=F32)

    def probabilities(kv, s):
        s = s.reshape(group, BLOCK, BAND)
        sink = jnp.concatenate(
            [jnp.full((1, BLOCK, BLOCK), sink_ref[kv * group + g] * inv_scale, F32)
             for g in range(group)], axis=0)
        s = jnp.concatenate([s, sink], axis=-1)
        m = jnp.max(s, axis=-1, keepdims=True)
        return jnp.exp2((s - m) * exp_coef).astype(BF16).reshape(group * BLOCK, BAND + BLOCK)

    def weighted_values(qb, kv, p, win0):
        rows = slice(qb * BLOCK, (qb + 1) * BLOCK)
        vw = v_ref[pl.ds(win0, BAND), kv * HEAD_DIM:(kv + 1) * HEAD_DIM]
        v_ext = jnp.concatenate([jnp.concatenate([vw, ones_cols], axis=1), sink_rows], axis=0)
        r = jnp.dot(p, v_ext, preferred_element_type=F32)
        og = r[:, :HEAD_DIM] / r[:, HEAD_DIM:]
        for g in range(group):
            hd = kv * group + g
            att_ref[rows, hd * HEAD_DIM:(hd + 1) * HEAD_DIM] = og[g * BLOCK:(g + 1) * BLOCK].astype(BF16)

    row_block = min(ATTN_ROW_BLOCK, tq)

    def project_rows(rb):
        rows = slice(rb * row_block, (rb + 1) * row_block)
        y = jnp.dot(att_ref[rows, :], wo_ref[...], preferred_element_type=F32)
        o_ref[rows, :] = x_ref[rows, :] + _rms(y, gpost_ref[...])

    steps = [(qb, kv) for qb in range(tq // BLOCK) for kv in range(N_KV_HEADS)]
    win = {qb: window(qb) for qb in range(tq // BLOCK)}
    steps_per_row_block = (row_block // BLOCK) * N_KV_HEADS
    s_cur = scores(*steps[0], *win[steps[0][0]])
    for idx, (qb, kv) in enumerate(steps):
        if idx + 1 < len(steps):
            nqb, nkv = steps[idx + 1]
            s_next = scores(nqb, nkv, *win[nqb])
        weighted_values(qb, kv, probabilities(kv, s_cur), win[qb][0])
        s_cur = s_next
        if (idx + 1) % steps_per_row_block == 0:
            project_rows(idx // steps_per_row_block)


def _attention(x, q, k, v, w_o, sink, g_post, *, layer, seq, tq):
    n, d = x.shape
    dq = q.shape[1]
    dk = k.shape[1]
    tps = seq // tq
    kern = functools.partial(_attn_kernel, seq=seq)
    grid_spec = pltpu.PrefetchScalarGridSpec(
        num_scalar_prefetch=1,
        grid=(n // seq, tps),
        in_specs=[
            pl.BlockSpec((tq, d), lambda b, i, s: (b * tps + i, 0)),
            pl.BlockSpec((tq, dq), lambda b, i, s: (b * tps + i, 0)),
            pl.BlockSpec((seq, dk), lambda b, i, s: (b, 0)),
            pl.BlockSpec((seq, dk), lambda b, i, s: (b, 0)),
            pl.BlockSpec((None, dq, d), lambda b, i, s: (layer, 0, 0), pipeline_mode=pl.Buffered(1)),
            pl.BlockSpec((1, d), lambda b, i, s: (0, 0)),
        ],
        out_specs=pl.BlockSpec((tq, d), lambda b, i, s: (b * tps + i, 0)),
        scratch_shapes=[pltpu.VMEM((tq, dq), BF16)],
    )
    return pl.pallas_call(
        kern,
        name="attention",
        grid_spec=grid_spec,
        out_shape=jax.ShapeDtypeStruct((n, d), F32),
        compiler_params=_params(2),
    )(sink, x, q, k, v, w_o, g_post)


def _rope_tables(seq):
    inv_freq = 1.0 / (ROPE_THETA ** (jnp.arange(0, HEAD_DIM, 2, dtype=F32) / HEAD_DIM))
    ang = jnp.arange(seq, dtype=F32)[:, None] * inv_freq[None, :]
    cos = jnp.cos(ang)
    sin = jnp.sin(ang)
    return jnp.concatenate([cos, cos], axis=-1), jnp.concatenate([-sin, sin], axis=-1)


def _trunk(x3, w, ffn_bf16, *, tm_ffn, tf, tm_conv, tc, tm_qkv, tq):
    b, seq, d = x3.shape
    x = x3.reshape(b * seq, d)
    depth = w["ffn_w_gate"].shape[0]
    cos, sin = _rope_tables(seq)
    for i in range(depth):
        j = i // 2
        g_pre = w["g_mix_pre"][i][None]
        g_post = w["g_mix_post"][i][None]
        if i % 2 == 0:
            x = _conv_mixer(x, g_pre, w["conv_w_in"], w["conv_w_dw"], w["conv_w_out"], g_post,
                            layer=j, seq=seq, tm=tm_conv, tc=tc)
        else:
            q, k, v = _qkv_rope(x, g_pre, w["attn_w_qkv"], cos, sin, layer=j, seq=seq, tm=tm_qkv)
            x = _attention(x, q, k, v, w["attn_w_o"], w["attn_sink"][j], g_post,
                           layer=j, seq=seq, tq=tq)
        cast_next = None
        if i + 1 < depth and i + 1 not in ffn_bf16:
            cast_next = (tuple(w[name] for name in FFN_WEIGHTS), i + 1)
        x, produced = _ffn(x, w["g_ffn_pre"][i][None], *ffn_bf16[i], w["g_ffn_post"][i][None],
                           tm=tm_ffn, tf=tf, cast_next=cast_next)
        if produced is not None:
            ffn_bf16[i + 1] = produced
    return x.reshape(b, seq, d)


def kernel(x_prompt, x_sample, conv_w_in, conv_w_dw, conv_w_out, attn_w_qkv, attn_w_o, attn_sink,
           ffn_w_gate, ffn_w_up, ffn_w_down, g_mix_pre, g_mix_post, g_ffn_pre, g_ffn_post):
    w = dict(
        conv_w_in=conv_w_in.astype(BF16), conv_w_dw=conv_w_dw, conv_w_out=conv_w_out.astype(BF16),
        attn_w_qkv=attn_w_qkv.astype(BF16), attn_w_o=attn_w_o.astype(BF16), attn_sink=attn_sink,
        ffn_w_gate=ffn_w_gate, ffn_w_up=ffn_w_up, ffn_w_down=ffn_w_down,
        g_mix_pre=g_mix_pre, g_mix_post=g_mix_post, g_ffn_pre=g_ffn_pre, g_ffn_post=g_ffn_post,
    )
    ffn_bf16 = {0: tuple(w[name][0].astype(BF16) for name in FFN_WEIGHTS)}
    tiles = dict(tm_ffn=1024, tf=512, tm_conv=1024, tc=512, tm_qkv=512, tq=512)
    y_prompt = _trunk(x_prompt, w, ffn_bf16, **tiles)
    y_sample = _trunk(x_sample, w, ffn_bf16, **tiles)
    return y_prompt, y_sample
```

```python
import functools

import jax
import jax.numpy as jnp
from jax import lax
from jax.experimental import pallas as pl
from jax.experimental.pallas import tpu as pltpu

EPS = 1e-6
NEG_INF = -1e30
LOG2E = 1.4426950408889634
ROPE_THETA = 10000.0
HEAD_DIM = 128
N_KV_HEADS = 4
WINDOW = 128
BLOCK = 128
BAND = 3 * BLOCK
CONV_WIDTH = 3
HALO = 16

BF16 = jnp.bfloat16
F32 = jnp.float32

VMEM_LIMIT_BYTES = 62 * 1024 * 1024


def _rms(x, g):
    r = lax.rsqrt(jnp.mean(x * x, axis=-1, keepdims=True) + EPS)
    return (x * r) * g


ROW_CHUNK = 128
FFN_ROW_BLOCK = 256
CONV_ROW_BLOCK = 256
ATTN_ROW_BLOCK = 256
FFN_WEIGHTS = ("ffn_w_gate", "ffn_w_up", "ffn_w_down")


def _for_row_chunks(n_rows, body):
    chunk = min(ROW_CHUNK, n_rows)

    def step(c, carry):
        body(pl.multiple_of(c * chunk, chunk), chunk)
        return carry

    lax.fori_loop(0, n_rows // chunk, step, 0, unroll=2)


def _prenorm_rows(dst_ref, dst_row0, src_ref, g_ref, n_rows):
    def body(r0, chunk):
        out = _rms(src_ref[pl.ds(r0, chunk), :], g_ref[...]).astype(dst_ref.dtype)
        dst_ref[pl.ds(pl.multiple_of(dst_row0 + r0, HALO), chunk), :] = out

    _for_row_chunks(n_rows, body)


def _postnorm_residual_rows(o_ref, x_ref, g_ref):
    def body(r0, chunk):
        rows = pl.ds(r0, chunk)
        o_ref[rows, :] = x_ref[rows, :] + _rms(o_ref[rows, :], g_ref[...])

    _for_row_chunks(o_ref.shape[0], body)


def _params(n_grid_dims):
    sem = ("parallel",) + ("arbitrary",) * (n_grid_dims - 1)
    return pltpu.CompilerParams(dimension_semantics=sem, vmem_limit_bytes=VMEM_LIMIT_BYTES)


def _pipelined(n_blocks, stage_a, stage_b):
    cur = stage_a(0)
    for i in range(n_blocks):
        nxt = stage_a(i + 1) if i + 1 < n_blocks else None
        stage_b(i, cur)
        cur = nxt


def _ffn_step(x_ref, gpre_ref, wg_ref, wu_ref, wd_ref, gpost_ref, o_ref, hn_ref, *, first, last,
              row_block):
    def rows(rb):
        return slice(rb * row_block, (rb + 1) * row_block)

    def gate_up(rb):
        if first:
            hn_ref[rows(rb), :] = _rms(x_ref[rows(rb), :], gpre_ref[...]).astype(BF16)
        hn = hn_ref[rows(rb), :]
        return (jnp.dot(hn, wg_ref[...], preferred_element_type=F32),
                jnp.dot(hn, wu_ref[...], preferred_element_type=F32))

    def down(rb, gate_and_up):
        gate, up = gate_and_up
        act = (jax.nn.silu(gate) * up).astype(BF16)
        part = jnp.dot(act, wd_ref[...], preferred_element_type=F32)
        if not first:
            part = o_ref[rows(rb), :] + part
        if last:
            part = x_ref[rows(rb), :] + _rms(part, gpost_ref[...])
        o_ref[rows(rb), :] = part

    _pipelined(x_ref.shape[0] // row_block, gate_up, down)


def _ffn_kernel(x_ref, gpre_ref, wg_ref, wu_ref, wd_ref, gpost_ref, *rest, row_block, n_cast):
    cast_src, o_ref, cast_dst, hn_ref = (rest[:n_cast], rest[n_cast],
                                         rest[n_cast + 1:2 * n_cast + 1], rest[-1])
    for src, dst in zip(cast_src, cast_dst):
        dst[...] = src[...].astype(dst.dtype)

    refs = (x_ref, gpre_ref, wg_ref, wu_ref, wd_ref, gpost_ref, o_ref, hn_ref)
    f = pl.program_id(1)
    last_f = pl.num_programs(1) - 1
    for first, last, cond in ((True, False, f == 0),
                              (False, False, (f > 0) & (f < last_f)),
                              (False, True, f == last_f)):
        pl.when(cond)(functools.partial(_ffn_step, *refs, first=first, last=last,
                                        row_block=row_block))


def _ffn(x, g_pre, w_gate, w_up, w_down, g_post, *, tm, tf, cast_next=None):
    n, d = x.shape
    f = w_gate.shape[1]
    nf = f // tf
    assert nf >= 2, "first and last d_ff chunk are separate code paths"
    n_tiles = n // tm
    cast_in, cast_specs_in, cast_specs_out, cast_shapes = [], [], [], []
    if cast_next is not None:
        stacked, nxt = cast_next
        for w32 in stacked:
            _, rows, cols = w32.shape
            if cols % (nf * 128) == 0:
                blk, index = (rows // n_tiles, cols // nf), (lambda i, j: (i, j))
            else:
                blk, index = (rows // (n_tiles * nf), cols), (lambda i, j: (i * nf + j, 0))
            assert blk[0] % HALO == 0 and rows % blk[0] == 0, "bf16 tiles are HALO rows"
            cast_in.append(w32)
            cast_specs_in.append(pl.BlockSpec(
                (None,) + blk, lambda i, j, index=index: (nxt,) + index(i, j)))
            cast_specs_out.append(pl.BlockSpec(blk, index))
            cast_shapes.append(jax.ShapeDtypeStruct((rows, cols), BF16))
    outs = pl.pallas_call(
        functools.partial(_ffn_kernel, row_block=min(FFN_ROW_BLOCK, tm), n_cast=len(cast_in)),
        name="ffn",
        grid=(n // tm, nf),
        in_specs=[
            pl.BlockSpec((tm, d), lambda i, j: (i, 0)),
            pl.BlockSpec((1, d), lambda i, j: (0, 0)),
            pl.BlockSpec((d, tf), lambda i, j: (0, j)),
            pl.BlockSpec((d, tf), lambda i, j: (0, j)),
            pl.BlockSpec((tf, d), lambda i, j: (j, 0)),
            pl.BlockSpec((1, d), lambda i, j: (0, 0)),
        ] + cast_specs_in,
        out_specs=[pl.BlockSpec((tm, d), lambda i, j: (i, 0))] + cast_specs_out,
        out_shape=[jax.ShapeDtypeStruct((n, d), F32)] + cast_shapes,
        scratch_shapes=[pltpu.VMEM((tm, d), BF16)],
        compiler_params=_params(2),
    )(x, g_pre, w_gate, w_up, w_down, g_post, *cast_in)
    return outs[0], (tuple(outs[1:]) if cast_next is not None else None)


def _conv_step(xm_ref, xp_ref, xn_ref, gpre_ref, wb_ref, wc_ref, wh_ref, wdw_ref, wo_ref,
               gpost_ref, o_ref, hn_ref, *, first, last, row_block, tiles_per_seq):
    tm = xm_ref.shape[0]
    nb = tm // row_block
    w = wdw_ref[...]

    def ext_rows(rb):
        lo = HALO + rb * row_block - (HALO if rb == 0 else 0)
        hi = HALO + (rb + 1) * row_block + (HALO if rb == nb - 1 else 0)
        return lo, hi

    def main_rows(rb):
        return slice(rb * row_block, (rb + 1) * row_block)

    def projections(rb):
        lo, hi = ext_rows(rb)
        if first:
            g = gpre_ref[...]
            hn_ref[HALO + rb * row_block:HALO + (rb + 1) * row_block, :] = (
                _rms(xm_ref[main_rows(rb), :], g).astype(BF16))
            t = pl.program_id(0) % tiles_per_seq
            if rb == 0:
                keep = (t != 0).astype(F32)
                hn_ref[0:HALO, :] = (_rms(xp_ref[...], g) * keep).astype(BF16)
            if rb == nb - 1:
                keep = (t != tiles_per_seq - 1).astype(F32)
                hn_ref[HALO + tm:, :] = (_rms(xn_ref[...], g) * keep).astype(BF16)
        hn = hn_ref[lo:hi, :]
        u = (jnp.dot(hn, wc_ref[...], preferred_element_type=F32)
             * jnp.dot(hn, wh_ref[...], preferred_element_type=F32))
        b = jnp.dot(hn_ref[HALO + rb * row_block:HALO + (rb + 1) * row_block, :], wb_ref[...],
                    preferred_element_type=F32)
        return u, b

    def mix_and_project(rb, u_blocks, b):
        u = u_blocks[rb]
        n_ext = u.shape[0]
        off = HALO if rb == 0 else 0
        u_prev = pltpu.roll(u, 1, 0)[off:off + row_block]
        u_next = pltpu.roll(u, n_ext - 1, 0)[off:off + row_block]
        row = lax.broadcasted_iota(jnp.int32, (row_block, 1), 0)
        if rb > 0:
            u_prev = jnp.where(row == 0, u_blocks[rb - 1][-1:], u_prev)
        if rb < nb - 1:
            u_next = jnp.where(row == row_block - 1, u_blocks[rb + 1][0:1], u_next)
        v = w[0:1] * u_prev + w[1:2] * u[off:off + row_block] + w[2:3] * u_next
        y = (b * v).astype(BF16)
        part = jnp.dot(y, wo_ref[...], preferred_element_type=F32)
        if not first:
            part = o_ref[main_rows(rb), :] + part
        if last:
            part = xm_ref[main_rows(rb), :] + _rms(part, gpost_ref[...])
        o_ref[main_rows(rb), :] = part

    u_blocks, b_blocks = {}, {}
    u_blocks[0], b_blocks[0] = projections(0)
    for rb in range(nb):
        if rb + 1 < nb:
            u_blocks[rb + 1], b_blocks[rb + 1] = projections(rb + 1)
        mix_and_project(rb, u_blocks, b_blocks[rb])


def _conv_kernel(*refs, row_block, tiles_per_seq):
    j = pl.program_id(1)
    last_j = pl.num_programs(1) - 1
    for first, last, cond in ((True, False, j == 0),
                              (False, False, (j > 0) & (j < last_j)),
                              (False, True, j == last_j)):
        pl.when(cond)(functools.partial(_conv_step, *refs, first=first, last=last,
                                        row_block=row_block, tiles_per_seq=tiles_per_seq))


def _conv_mixer(x, g_pre, w_in, w_dw, w_out, g_post, *, layer, seq, tm, tc):
    n, d = x.shape
    nc = d // tc
    assert nc >= 2, "first and last channel chunk are separate code paths"
    hb = tm // HALO
    n_hb = n // HALO
    kern = functools.partial(_conv_kernel, row_block=min(CONV_ROW_BLOCK, tm),
                             tiles_per_seq=seq // tm)
    return pl.pallas_call(
        kern,
        name="conv_mixer",
        grid=(n // tm, nc),
        in_specs=[
            pl.BlockSpec((tm, d), lambda i, j: (i, 0)),
            pl.BlockSpec((HALO, d), lambda i, j: (jnp.maximum(i * hb - 1, 0), 0)),
            pl.BlockSpec((HALO, d), lambda i, j: (jnp.minimum((i + 1) * hb, n_hb - 1), 0)),
            pl.BlockSpec((1, d), lambda i, j: (0, 0)),
            pl.BlockSpec((None, d, tc), lambda i, j: (layer, 0, j)),
            pl.BlockSpec((None, d, tc), lambda i, j: (layer, 0, nc + j)),
            pl.BlockSpec((None, d, tc), lambda i, j: (layer, 0, 2 * nc + j)),
            pl.BlockSpec((None, CONV_WIDTH, tc), lambda i, j: (layer, 0, j)),
            pl.BlockSpec((None, tc, d), lambda i, j: (layer, j, 0)),
            pl.BlockSpec((1, d), lambda i, j: (0, 0)),
        ],
        out_specs=pl.BlockSpec((tm, d), lambda i, j: (i, 0)),
        out_shape=jax.ShapeDtypeStruct((n, d), F32),
        scratch_shapes=[pltpu.VMEM((tm + 2 * HALO, d), BF16)],
        compiler_params=_params(2),
    )(x, x, x, g_pre, w_in, w_in, w_in, w_dw, w_out, g_post)


def _qkv_kernel(x_ref, gpre_ref, w_ref, cos_ref, sin_ref, q_ref, k_ref, v_ref):
    dq = q_ref.shape[1]
    dk = k_ref.shape[1]
    hn = _rms(x_ref[...], gpre_ref[...]).astype(BF16)
    qkv = jnp.dot(hn, w_ref[...], preferred_element_type=F32)
    cos = cos_ref[...]
    sin = sin_ref[...]

    def rope(xh):
        return xh * cos + pltpu.roll(xh, HEAD_DIM // 2, 1) * sin

    for hd in range(dq // HEAD_DIM):
        sl = slice(hd * HEAD_DIM, (hd + 1) * HEAD_DIM)
        q_ref[:, sl] = rope(qkv[:, sl]).astype(BF16)
    for hd in range(dk // HEAD_DIM):
        sl = slice(hd * HEAD_DIM, (hd + 1) * HEAD_DIM)
        k_ref[:, sl] = rope(qkv[:, dq + hd * HEAD_DIM:dq + (hd + 1) * HEAD_DIM]).astype(BF16)
    v_ref[...] = qkv[:, dq + dk:].astype(BF16)


def _qkv_rope(x, g_pre, w_qkv, cos, sin, *, layer, seq, tm):
    n, d = x.shape
    dqkv = w_qkv.shape[2]
    dk = N_KV_HEADS * HEAD_DIM
    dq = dqkv - 2 * dk
    tps = seq // tm
    return pl.pallas_call(
        _qkv_kernel,
        name="qkv_rope",
        grid=(n // tm,),
        in_specs=[
            pl.BlockSpec((tm, d), lambda i: (i, 0)),
            pl.BlockSpec((1, d), lambda i: (0, 0)),
            pl.BlockSpec((None, d, dqkv), lambda i: (layer, 0, 0), pipeline_mode=pl.Buffered(1)),
            pl.BlockSpec((tm, HEAD_DIM), lambda i: (i % tps, 0)),
            pl.BlockSpec((tm, HEAD_DIM), lambda i: (i % tps, 0)),
        ],
        out_specs=[
            pl.BlockSpec((tm, dq), lambda i: (i, 0)),
            pl.BlockSpec((tm, dk), lambda i: (i, 0)),
            pl.BlockSpec((tm, dk), lambda i: (i, 0)),
        ],
        out_shape=[
            jax.ShapeDtypeStruct((n, dq), BF16),
            jax.ShapeDtypeStruct((n, dk), BF16),
            jax.ShapeDtypeStruct((n, dk), BF16),
        ],
        compiler_params=_params(1),
    )(x, g_pre, w_qkv, cos, sin)


def _attn_kernel(sink_ref, x_ref, q_ref, k_ref, v_ref, wo_ref, gpost_ref, o_ref, att_ref, *, seq):
    qi = pl.program_id(1)
    tq = q_ref.shape[0]
    n_heads = q_ref.shape[1] // HEAD_DIM
    group = n_heads // N_KV_HEADS
    exp_coef = HEAD_DIM ** -0.5 * LOG2E
    inv_scale = HEAD_DIM ** 0.5
    qrow_minus_krow = (lax.broadcasted_iota(jnp.int32, (BAND, BLOCK), 1)
                       - lax.broadcasted_iota(jnp.int32, (BAND, BLOCK), 0))
    one_hot_rows = (lax.broadcasted_iota(jnp.int32, (BLOCK, BLOCK), 0)
                    == lax.broadcasted_iota(jnp.int32, (BLOCK, BLOCK), 1))
    one_hot_rows = jnp.where(one_hot_rows, 1.0, 0.0).astype(BF16)
    ones_cols = jnp.ones((BAND, HEAD_DIM), BF16)
    sink_row = ((lax.broadcasted_iota(jnp.int32, (BLOCK, 2 * HEAD_DIM), 0) == 0)
                & (lax.broadcasted_iota(jnp.int32, (BLOCK, 2 * HEAD_DIM), 1) >= HEAD_DIM))
    sink_rows = jnp.where(sink_row, 1.0, 0.0).astype(BF16)

    def window(qb):
        qpos0 = qi * tq + qb * BLOCK
        win0 = pl.multiple_of(jnp.clip(qpos0 - BLOCK, 0, seq - BAND), BLOCK)
        diff = qrow_minus_krow + (qpos0 - win0)
        valid = (diff <= WINDOW) & (diff >= -WINDOW)
        return win0, jnp.where(valid, 0.0, NEG_INF).astype(BF16)

    def scores(qb, kv, win0, mask):
        rows = slice(qb * BLOCK, (qb + 1) * BLOCK)
        kw = k_ref[pl.ds(win0, BAND), kv * HEAD_DIM:(kv + 1) * HEAD_DIM]
        k_ext = jnp.concatenate([kw, mask], axis=1)
        q_ext = jnp.concatenate(
            [jnp.concatenate(
                [q_ref[rows, (kv * group + g) * HEAD_DIM:(kv * group + g + 1) * HEAD_DIM],
                 one_hot_rows], axis=1)
             for g in range(group)], axis=0)
        return lax.dot_general(q_ext, k_ext, (((1,), (1,)), ((), ())), preferred_element_type=F32)

    def probabilities(kv, s):
        s = s.reshape(group, BLOCK, BAND)
        sink = jnp.concatenate(
            [jnp.full((1, BLOCK, BLOCK), sink_ref[kv * group + g] * inv_scale, F32)
             for g in range(group)], axis=0)
        s = jnp.concatenate([s, sink], axis=-1)
        m = jnp.max(s, axis=-1, keepdims=True)
        return jnp.exp2((s - m) * exp_coef).astype(BF16).reshape(group * BLOCK, BAND + BLOCK)

    def weighted_values(qb, kv, p, win0):
        rows = slice(qb * BLOCK, (qb + 1) * BLOCK)
        vw = v_ref[pl.ds(win0, BAND), kv * HEAD_DIM:(kv + 1) * HEAD_DIM]
        v_ext = jnp.concatenate([jnp.concatenate([vw, ones_cols], axis=1), sink_rows], axis=0)
        r = jnp.dot(p, v_ext, preferred_element_type=F32)
        og = r[:, :HEAD_DIM] / r[:, HEAD_DIM:]
        for g in range(group):
            hd = kv * group + g
            att_ref[rows, hd * HEAD_DIM:(hd + 1) * HEAD_DIM] = og[g * BLOCK:(g + 1) * BLOCK].astype(BF16)

    row_block = min(ATTN_ROW_BLOCK, tq)

    def project_rows(rb):
        rows = slice(rb * row_block, (rb + 1) * row_block)
        y = jnp.dot(att_ref[rows, :], wo_ref[...], preferred_element_type=F32)
        o_ref[rows, :] = x_ref[rows, :] + _rms(y, gpost_ref[...])

    steps = [(qb, kv) for qb in range(tq // BLOCK) for kv in range(N_KV_HEADS)]
    win = {qb: window(qb) for qb in range(tq // BLOCK)}
    steps_per_row_block = (row_block // BLOCK) * N_KV_HEADS
    s_cur = scores(*steps[0], *win[steps[0][0]])
    for idx, (qb, kv) in enumerate(steps):
        if idx + 1 < len(steps):
            nqb, nkv = steps[idx + 1]
            s_next = scores(nqb, nkv, *win[nqb])
        weighted_values(qb, kv, probabilities(kv, s_cur), win[qb][0])
        s_cur = s_next
        if (idx + 1) % steps_per_row_block == 0:
            project_rows(idx // steps_per_row_block)


def _attention(x, q, k, v, w_o, sink, g_post, *, layer, seq, tq):
    n, d = x.shape
    dq = q.shape[1]
    dk = k.shape[1]
    tps = seq // tq
    kern = functools.partial(_attn_kernel, seq=seq)
    grid_spec = pltpu.PrefetchScalarGridSpec(
        num_scalar_prefetch=1,
        grid=(n // seq, tps),
        in_specs=[
            pl.BlockSpec((tq, d), lambda b, i, s: (b * tps + i, 0)),
            pl.BlockSpec((tq, dq), lambda b, i, s: (b * tps + i, 0)),
            pl.BlockSpec((seq, dk), lambda b, i, s: (b, 0)),
            pl.BlockSpec((seq, dk), lambda b, i, s: (b, 0)),
            pl.BlockSpec((None, dq, d), lambda b, i, s: (layer, 0, 0), pipeline_mode=pl.Buffered(1)),
            pl.BlockSpec((1, d), lambda b, i, s: (0, 0)),
        ],
        out_specs=pl.BlockSpec((tq, d), lambda b, i, s: (b * tps + i, 0)),
        scratch_shapes=[pltpu.VMEM((tq, dq), BF16)],
    )
    return pl.pallas_call(
        kern,
        name="attention",
        grid_spec=grid_spec,
        out_shape=jax.ShapeDtypeStruct((n, d), F32),
        compiler_params=_params(2),
    )(sink, x, q, k, v, w_o, g_post)


def _rope_tables(seq):
    inv_freq = 1.0 / (ROPE_THETA ** (jnp.arange(0, HEAD_DIM, 2, dtype=F32) / HEAD_DIM))
    ang = jnp.arange(seq, dtype=F32)[:, None] * inv_freq[None, :]
    cos = jnp.cos(ang)
    sin = jnp.sin(ang)
    return jnp.concatenate([cos, cos], axis=-1), jnp.concatenate([-sin, sin], axis=-1)


def _trunk(x3, w, ffn_bf16, *, tm_ffn, tf, tm_conv, tc, tm_qkv, tq):
    b, seq, d = x3.shape
    x = x3.reshape(b * seq, d)
    depth = w["ffn_w_gate"].shape[0]
    cos, sin = _rope_tables(seq)
    for i in range(depth):
        j = i // 2
        g_pre = w["g_mix_pre"][i][None]
        g_post = w["g_mix_post"][i][None]
        if i % 2 == 0:
            x = _conv_mixer(x, g_pre, w["conv_w_in"], w["conv_w_dw"], w["conv_w_out"], g_post,
                            layer=j, seq=seq, tm=tm_conv, tc=tc)
        else:
            q, k, v = _qkv_rope(x, g_pre, w["attn_w_qkv"], cos, sin, layer=j, seq=seq, tm=tm_qkv)
            x = _attention(x, q, k, v, w["attn_w_o"], w["attn_sink"][j], g_post,
                           layer=j, seq=seq, tq=tq)
        cast_next = None
        if i + 1 < depth and i + 1 not in ffn_bf16:
            cast_next = (tuple(w[name] for name in FFN_WEIGHTS), i + 1)
        x, produced = _ffn(x, w["g_ffn_pre"][i][None], *ffn_bf16[i], w["g_ffn_post"][i][None],
                           tm=tm_ffn, tf=tf, cast_next=cast_next)
        if produced is not None:
            ffn_bf16[i + 1] = produced
    return x.reshape(b, seq, d)


def kernel(x_prompt, x_sample, conv_w_in, conv_w_dw, conv_w_out, attn_w_qkv, attn_w_o, attn_sink,
           ffn_w_gate, ffn_w_up, ffn_w_down, g_mix_pre, g_mix_post, g_ffn_pre, g_ffn_post):
    w = dict(
        conv_w_in=conv_w_in.astype(BF16), conv_w_dw=conv_w_dw, conv_w_out=conv_w_out.astype(BF16),
        attn_w_qkv=attn_w_qkv.astype(BF16), attn_w_o=attn_w_o.astype(BF16), attn_sink=attn_sink,
        ffn_w_gate=ffn_w_gate, ffn_w_up=ffn_w_up, ffn_w_down=ffn_w_down,
        g_mix_pre=g_mix_pre, g_mix_post=g_mix_post, g_ffn_pre=g_ffn_pre, g_ffn_post=g_ffn_post,
    )
    ffn_bf16 = {0: tuple(w[name][0].astype(BF16) for name in FFN_WEIGHTS)}
    tiles = dict(tm_ffn=1024, tf=512, tm_conv=1024, tc=512, tm_qkv=512, tq=512)
    y_prompt = _trunk(x_prompt, w, ffn_bf16, **tiles)
    y_sample = _trunk(x_sample, w, ffn_bf16, **tiles)
    return y_prompt, y_sample
```

```python
import functools

import jax
import jax.numpy as jnp
from jax import lax
from jax.experimental import pallas as pl
from jax.experimental.pallas import tpu as pltpu

EPS = 1e-6
NEG_INF = -1e30
LOG2E = 1.4426950408889634
ROPE_THETA = 10000.0
HEAD_DIM = 128
N_KV_HEADS = 4
WINDOW = 128
BLOCK = 128
BAND = 3 * BLOCK
CONV_WIDTH = 3
HALO = 16

BF16 = jnp.bfloat16
F32 = jnp.float32

VMEM_LIMIT_BYTES = 62 * 1024 * 1024


def _rms(x, g):
    r = lax.rsqrt(jnp.mean(x * x, axis=-1, keepdims=True) + EPS)
    return (x * r) * g


ROW_CHUNK = 128
FFN_ROW_BLOCK = 256
CONV_ROW_BLOCK = 256
ATTN_ROW_BLOCK = 256
FFN_WEIGHTS = ("ffn_w_gate", "ffn_w_up", "ffn_w_down")


def _for_row_chunks(n_rows, body):
    chunk = min(ROW_CHUNK, n_rows)

    def step(c, carry):
        body(pl.multiple_of(c * chunk, chunk), chunk)
        return carry

    lax.fori_loop(0, n_rows // chunk, step, 0, unroll=2)


def _prenorm_rows(dst_ref, dst_row0, src_ref, g_ref, n_rows):
    def body(r0, chunk):
        out = _rms(src_ref[pl.ds(r0, chunk), :], g_ref[...]).astype(dst_ref.dtype)
        dst_ref[pl.ds(pl.multiple_of(dst_row0 + r0, HALO), chunk), :] = out

    _for_row_chunks(n_rows, body)


def _postnorm_residual_rows(o_ref, x_ref, g_ref):
    def body(r0, chunk):
        rows = pl.ds(r0, chunk)
        o_ref[rows, :] = x_ref[rows, :] + _rms(o_ref[rows, :], g_ref[...])

    _for_row_chunks(o_ref.shape[0], body)


def _params(n_grid_dims):
    sem = ("parallel",) + ("arbitrary",) * (n_grid_dims - 1)
    return pltpu.CompilerParams(dimension_semantics=sem, vmem_limit_bytes=VMEM_LIMIT_BYTES)


def _pipelined(n_blocks, stage_a, stage_b):
    cur = stage_a(0)
    for i in range(n_blocks):
        nxt = stage_a(i + 1) if i + 1 < n_blocks else None
        stage_b(i, cur)
        cur = nxt


def _ffn_step(x_ref, gpre_ref, wg_ref, wu_ref, wd_ref, gpost_ref, o_ref, hn_ref, *, first, last,
              row_block):
    def rows(rb):
        return slice(rb * row_block, (rb + 1) * row_block)

    def gate_up(rb):
        if first:
            hn_ref[rows(rb), :] = _rms(x_ref[rows(rb), :], gpre_ref[...]).astype(BF16)
        hn = hn_ref[rows(rb), :]
        return (jnp.dot(hn, wg_ref[...], preferred_element_type=F32),
                jnp.dot(hn, wu_ref[...], preferred_element_type=F32))

    def down(rb, gate_and_up):
        gate, up = gate_and_up
        act = (jax.nn.silu(gate) * up).astype(BF16)
        part = jnp.dot(act, wd_ref[...], preferred_element_type=F32)
        if not first:
            part = o_ref[rows(rb), :] + part
        if last:
            part = x_ref[rows(rb), :] + _rms(part, gpost_ref[...])
        o_ref[rows(rb), :] = part

    _pipelined(x_ref.shape[0] // row_block, gate_up, down)


def _ffn_kernel(x_ref, gpre_ref, wg_ref, wu_ref, wd_ref, gpost_ref, *rest, row_block, n_cast):
    cast_src, o_ref, cast_dst, hn_ref = (rest[:n_cast], rest[n_cast],
                                         rest[n_cast + 1:2 * n_cast + 1], rest[-1])
    for src, dst in zip(cast_src, cast_dst):
        dst[...] = src[...].astype(dst.dtype)

    refs = (x_ref, gpre_ref, wg_ref, wu_ref, wd_ref, gpost_ref, o_ref, hn_ref)
    f = pl.program_id(1)
    last_f = pl.num_programs(1) - 1
    for first, last, cond in ((True, False, f == 0),
                              (False, False, (f > 0) & (f < last_f)),
                              (False, True, f == last_f)):
        pl.when(cond)(functools.partial(_ffn_step, *refs, first=first, last=last,
                                        row_block=row_block))


def _ffn(x, g_pre, w_gate, w_up, w_down, g_post, *, tm, tf, cast_next=None):
    n, d = x.shape
    nf = w_gate.shape[0]
    assert w_gate.shape == (nf, d, tf)
    assert nf >= 2, "first and last d_ff chunk are separate code paths"
    n_tiles = n // tm
    cast_in, cast_specs_in, cast_specs_out, cast_shapes = [], [], [], []
    if cast_next is not None:
        stacked, nxt = cast_next
        for w32 in stacked:
            _, rows, cols = w32.shape
            if cols == nf * tf:
                blk = (rows // n_tiles, tf)
                src_index = lambda i, j: (nxt, i, j)
                dst_spec = pl.BlockSpec((None,) + blk, lambda i, j: (j, i, 0))
                dst_shape = (nf, rows, tf)
            else:
                blk = (rows // (n_tiles * nf), cols)
                src_index = lambda i, j: (nxt, i * nf + j, 0)
                dst_spec = pl.BlockSpec(blk, lambda i, j: (i * nf + j, 0))
                dst_shape = (rows, cols)
            assert blk[0] % HALO == 0 and rows % blk[0] == 0, "bf16 tiles are HALO rows"
            cast_in.append(w32)
            cast_specs_in.append(pl.BlockSpec((None,) + blk, src_index))
            cast_specs_out.append(dst_spec)
            cast_shapes.append(jax.ShapeDtypeStruct(dst_shape, BF16))
    outs = pl.pallas_call(
        functools.partial(_ffn_kernel, row_block=min(FFN_ROW_BLOCK, tm), n_cast=len(cast_in)),
        name="ffn",
        grid=(n // tm, nf),
        in_specs=[
            pl.BlockSpec((tm, d), lambda i, j: (i, 0)),
            pl.BlockSpec((1, d), lambda i, j: (0, 0)),
            pl.BlockSpec((None, d, tf), lambda i, j: (j, 0, 0)),
            pl.BlockSpec((None, d, tf), lambda i, j: (j, 0, 0)),
            pl.BlockSpec((tf, d), lambda i, j: (j, 0)),
            pl.BlockSpec((1, d), lambda i, j: (0, 0)),
        ] + cast_specs_in,
        out_specs=[pl.BlockSpec((tm, d), lambda i, j: (i, 0))] + cast_specs_out,
        out_shape=[jax.ShapeDtypeStruct((n, d), F32)] + cast_shapes,
        scratch_shapes=[pltpu.VMEM((tm, d), BF16)],
        compiler_params=_params(2),
    )(x, g_pre, w_gate, w_up, w_down, g_post, *cast_in)
    return outs[0], (tuple(outs[1:]) if cast_next is not None else None)


def _conv_step(xm_ref, xp_ref, xn_ref, gpre_ref, wb_ref, wc_ref, wh_ref, wdw_ref, wo_ref,
               gpost_ref, o_ref, hn_ref, *, first, last, row_block, tiles_per_seq):
    tm = xm_ref.shape[0]
    nb = tm // row_block
    w = wdw_ref[...]

    def ext_rows(rb):
        lo = HALO + rb * row_block - (HALO if rb == 0 else 0)
        hi = HALO + (rb + 1) * row_block + (HALO if rb == nb - 1 else 0)
        return lo, hi

    def main_rows(rb):
        return slice(rb * row_block, (rb + 1) * row_block)

    def projections(rb):
        lo, hi = ext_rows(rb)
        if first:
            g = gpre_ref[...]
            hn_ref[HALO + rb * row_block:HALO + (rb + 1) * row_block, :] = (
                _rms(xm_ref[main_rows(rb), :], g).astype(BF16))
            t = pl.program_id(0) % tiles_per_seq
            if rb == 0:
                keep = (t != 0).astype(F32)
                hn_ref[0:HALO, :] = (_rms(xp_ref[...], g) * keep).astype(BF16)
            if rb == nb - 1:
                keep = (t != tiles_per_seq - 1).astype(F32)
                hn_ref[HALO + tm:, :] = (_rms(xn_ref[...], g) * keep).astype(BF16)
        hn = hn_ref[lo:hi, :]
        u = (jnp.dot(hn, wc_ref[...], preferred_element_type=F32)
             * jnp.dot(hn, wh_ref[...], preferred_element_type=F32))
        b = jnp.dot(hn_ref[HALO + rb * row_block:HALO + (rb + 1) * row_block, :], wb_ref[...],
                    preferred_element_type=F32)
        return u, b

    def mix_and_project(rb, u_blocks, b):
        u = u_blocks[rb]
        n_ext = u.shape[0]
        off = HALO if rb == 0 else 0
        u_prev = pltpu.roll(u, 1, 0)[off:off + row_block]
        u_next = pltpu.roll(u, n_ext - 1, 0)[off:off + row_block]
        row = lax.broadcasted_iota(jnp.int32, (row_block, 1), 0)
        if rb > 0:
            u_prev = jnp.where(row == 0, u_blocks[rb - 1][-1:], u_prev)
        if rb < nb - 1:
            u_next = jnp.where(row == row_block - 1, u_blocks[rb + 1][0:1], u_next)
        v = w[0:1] * u_prev + w[1:2] * u[off:off + row_block] + w[2:3] * u_next
        y = (b * v).astype(BF16)
        part = jnp.dot(y, wo_ref[...], preferred_element_type=F32)
        if not first:
            part = o_ref[main_rows(rb), :] + part
        if last:
            part = xm_ref[main_rows(rb), :] + _rms(part, gpost_ref[...])
        o_ref[main_rows(rb), :] = part

    u_blocks, b_blocks = {}, {}
    u_blocks[0], b_blocks[0] = projections(0)
    for rb in range(nb):
        if rb + 1 < nb:
            u_blocks[rb + 1], b_blocks[rb + 1] = projections(rb + 1)
        mix_and_project(rb, u_blocks, b_blocks[rb])


def _conv_kernel(*refs, row_block, tiles_per_seq):
    j = pl.program_id(1)
    last_j = pl.num_programs(1) - 1
    for first, last, cond in ((True, False, j == 0),
                              (False, False, (j > 0) & (j < last_j)),
                              (False, True, j == last_j)):
        pl.when(cond)(functools.partial(_conv_step, *refs, first=first, last=last,
                                        row_block=row_block, tiles_per_seq=tiles_per_seq))


def _conv_mixer(x, g_pre, w_in, w_dw, w_out, g_post, *, layer, seq, tm, tc):
    n, d = x.shape
    nc = d // tc
    assert nc >= 2, "first and last channel chunk are separate code paths"
    assert w_in.shape[1:] == (3 * nc, d, tc)
    hb = tm // HALO
    n_hb = n // HALO
    kern = functools.partial(_conv_kernel, row_block=min(CONV_ROW_BLOCK, tm),
                             tiles_per_seq=seq // tm)
    return pl.pallas_call(
        kern,
        name="conv_mixer",
        grid=(n // tm, nc),
        in_specs=[
            pl.BlockSpec((tm, d), lambda i, j: (i, 0)),
            pl.BlockSpec((HALO, d), lambda i, j: (jnp.maximum(i * hb - 1, 0), 0)),
            pl.BlockSpec((HALO, d), lambda i, j: (jnp.minimum((i + 1) * hb, n_hb - 1), 0)),
            pl.BlockSpec((1, d), lambda i, j: (0, 0)),
            pl.BlockSpec((None, None, d, tc), lambda i, j: (layer, j, 0, 0)),
            pl.BlockSpec((None, None, d, tc), lambda i, j: (layer, nc + j, 0, 0)),
            pl.BlockSpec((None, None, d, tc), lambda i, j: (layer, 2 * nc + j, 0, 0)),
            pl.BlockSpec((None, CONV_WIDTH, tc), lambda i, j: (layer, 0, j)),
            pl.BlockSpec((None, tc, d), lambda i, j: (layer, j, 0)),
            pl.BlockSpec((1, d), lambda i, j: (0, 0)),
        ],
        out_specs=pl.BlockSpec((tm, d), lambda i, j: (i, 0)),
        out_shape=jax.ShapeDtypeStruct((n, d), F32),
        scratch_shapes=[pltpu.VMEM((tm + 2 * HALO, d), BF16)],
        compiler_params=_params(2),
    )(x, x, x, g_pre, w_in, w_in, w_in, w_dw, w_out, g_post)


def _qkv_kernel(x_ref, gpre_ref, w_ref, cos_ref, sin_ref, q_ref, k_ref, v_ref):
    dq = q_ref.shape[1]
    dk = k_ref.shape[1]
    hn = _rms(x_ref[...], gpre_ref[...]).astype(BF16)
    qkv = jnp.dot(hn, w_ref[...], preferred_element_type=F32)
    cos = cos_ref[...]
    sin = sin_ref[...]

    def rope(xh):
        return xh * cos + pltpu.roll(xh, HEAD_DIM // 2, 1) * sin

    for hd in range(dq // HEAD_DIM):
        sl = slice(hd * HEAD_DIM, (hd + 1) * HEAD_DIM)
        q_ref[:, sl] = rope(qkv[:, sl]).astype(BF16)
    for hd in range(dk // HEAD_DIM):
        sl = slice(hd * HEAD_DIM, (hd + 1) * HEAD_DIM)
        k_ref[:, sl] = rope(qkv[:, dq + hd * HEAD_DIM:dq + (hd + 1) * HEAD_DIM]).astype(BF16)
    v_ref[...] = qkv[:, dq + dk:].astype(BF16)


def _qkv_rope(x, g_pre, w_qkv, cos, sin, *, layer, seq, tm):
    n, d = x.shape
    dqkv = w_qkv.shape[2]
    dk = N_KV_HEADS * HEAD_DIM
    dq = dqkv - 2 * dk
    tps = seq // tm
    return pl.pallas_call(
        _qkv_kernel,
        name="qkv_rope",
        grid=(n // tm,),
        in_specs=[
            pl.BlockSpec((tm, d), lambda i: (i, 0)),
            pl.BlockSpec((1, d), lambda i: (0, 0)),
            pl.BlockSpec((None, d, dqkv), lambda i: (layer, 0, 0), pipeline_mode=pl.Buffered(1)),
            pl.BlockSpec((tm, HEAD_DIM), lambda i: (i % tps, 0)),
            pl.BlockSpec((tm, HEAD_DIM), lambda i: (i % tps, 0)),
        ],
        out_specs=[
            pl.BlockSpec((tm, dq), lambda i: (i, 0)),
            pl.BlockSpec((tm, dk), lambda i: (i, 0)),
            pl.BlockSpec((tm, dk), lambda i: (i, 0)),
        ],
        out_shape=[
            jax.ShapeDtypeStruct((n, dq), BF16),
            jax.ShapeDtypeStruct((n, dk), BF16),
            jax.ShapeDtypeStruct((n, dk), BF16),
        ],
        compiler_params=_params(1),
    )(x, g_pre, w_qkv, cos, sin)


def _attn_kernel(sink_ref, x_ref, q_ref, k_ref, v_ref, wo_ref, gpost_ref, o_ref, att_ref, *, seq):
    qi = pl.program_id(1)
    tq = q_ref.shape[0]
    n_heads = q_ref.shape[1] // HEAD_DIM
    group = n_heads // N_KV_HEADS
    exp_coef = HEAD_DIM ** -0.5 * LOG2E
    inv_scale = HEAD_DIM ** 0.5
    qrow_minus_krow = (lax.broadcasted_iota(jnp.int32, (BAND, BLOCK), 1)
                       - lax.broadcasted_iota(jnp.int32, (BAND, BLOCK), 0))
    one_hot_rows = (lax.broadcasted_iota(jnp.int32, (BLOCK, BLOCK), 0)
                    == lax.broadcasted_iota(jnp.int32, (BLOCK, BLOCK), 1))
    one_hot_rows = jnp.where(one_hot_rows, 1.0, 0.0).astype(BF16)
    ones_cols = jnp.ones((BAND, HEAD_DIM), BF16)
    sink_row = ((lax.broadcasted_iota(jnp.int32, (BLOCK, 2 * HEAD_DIM), 0) == 0)
                & (lax.broadcasted_iota(jnp.int32, (BLOCK, 2 * HEAD_DIM), 1) >= HEAD_DIM))
    sink_rows = jnp.where(sink_row, 1.0, 0.0).astype(BF16)

    def window(qb):
        qpos0 = qi * tq + qb * BLOCK
        win0 = pl.multiple_of(jnp.clip(qpos0 - BLOCK, 0, seq - BAND), BLOCK)
        diff = qrow_minus_krow + (qpos0 - win0)
        valid = (diff <= WINDOW) & (diff >= -WINDOW)
        return win0, jnp.where(valid, 0.0, NEG_INF).astype(BF16)

    def scores(qb, kv, win0, mask):
        rows = slice(qb * BLOCK, (qb + 1) * BLOCK)
        kw = k_ref[pl.ds(win0, BAND), kv * HEAD_DIM:(kv + 1) * HEAD_DIM]
        k_ext = jnp.concatenate([kw, mask], axis=1)
        q_ext = jnp.concatenate(
            [jnp.concatenate(
                [q_ref[rows, (kv * group + g) * HEAD_DIM:(kv * group + g + 1) * HEAD_DIM],
                 one_hot_rows], axis=1)
             for g in range(group)], axis=0)
        return lax.dot_general(q_ext, k_ext, (((1,), (1,)), ((), ())), preferred_element_type=F32)

    def probabilities(kv, s):
        s = s.reshape(group, BLOCK, BAND)
        sink = jnp.concatenate(
            [jnp.full((1, BLOCK, BLOCK), sink_ref[kv * group + g] * inv_scale, F32)
             for g in range(group)], axis=0)
        s = jnp.concatenate([s, sink], axis=-1)
        m = jnp.max(s, axis=-1, keepdims=True)
        return jnp.exp2((s - m) * exp_coef).astype(BF16).reshape(group * BLOCK, BAND + BLOCK)

    def weighted_values(qb, kv, p, win0):
        rows = slice(qb * BLOCK, (qb + 1) * BLOCK)
        vw = v_ref[pl.ds(win0, BAND), kv * HEAD_DIM:(kv + 1) * HEAD_DIM]
        v_ext = jnp.concatenate([jnp.concatenate([vw, ones_cols], axis=1), sink_rows], axis=0)
        r = jnp.dot(p, v_ext, preferred_element_type=F32)
        og = r[:, :HEAD_DIM] / r[:, HEAD_DIM:]
        for g in range(group):
            hd = kv * group + g
            att_ref[rows, hd * HEAD_DIM:(hd + 1) * HEAD_DIM] = og[g * BLOCK:(g + 1) * BLOCK].astype(BF16)

    row_block = min(ATTN_ROW_BLOCK, tq)

    def project_rows(rb):
        rows = slice(rb * row_block, (rb + 1) * row_block)
        y = jnp.dot(att_ref[rows, :], wo_ref[...], preferred_element_type=F32)
        o_ref[rows, :] = x_ref[rows, :] + _rms(y, gpost_ref[...])

    steps = [(qb, kv) for qb in range(tq // BLOCK) for kv in range(N_KV_HEADS)]
    win = {qb: window(qb) for qb in range(tq // BLOCK)}
    steps_per_row_block = (row_block // BLOCK) * N_KV_HEADS
    s_cur = scores(*steps[0], *win[steps[0][0]])
    for idx, (qb, kv) in enumerate(steps):
        if idx + 1 < len(steps):
            nqb, nkv = steps[idx + 1]
            s_next = scores(nqb, nkv, *win[nqb])
        weighted_values(qb, kv, probabilities(kv, s_cur), win[qb][0])
        s_cur = s_next
        if (idx + 1) % steps_per_row_block == 0:
            project_rows(idx // steps_per_row_block)


def _attention(x, q, k, v, w_o, sink, g_post, *, layer, seq, tq):
    n, d = x.shape
    dq = q.shape[1]
    dk = k.shape[1]
    tps = seq // tq
    kern = functools.partial(_attn_kernel, seq=seq)
    grid_spec = pltpu.PrefetchScalarGridSpec(
        num_scalar_prefetch=1,
        grid=(n // seq, tps),
        in_specs=[
            pl.BlockSpec((tq, d), lambda b, i, s: (b * tps + i, 0)),
            pl.BlockSpec((tq, dq), lambda b, i, s: (b * tps + i, 0)),
            pl.BlockSpec((seq, dk), lambda b, i, s: (b, 0)),
            pl.BlockSpec((seq, dk), lambda b, i, s: (b, 0)),
            pl.BlockSpec((None, dq, d), lambda b, i, s: (layer, 0, 0), pipeline_mode=pl.Buffered(1)),
            pl.BlockSpec((1, d), lambda b, i, s: (0, 0)),
        ],
        out_specs=pl.BlockSpec((tq, d), lambda b, i, s: (b * tps + i, 0)),
        scratch_shapes=[pltpu.VMEM((tq, dq), BF16)],
    )
    return pl.pallas_call(
        kern,
        name="attention",
        grid_spec=grid_spec,
        out_shape=jax.ShapeDtypeStruct((n, d), F32),
        compiler_params=_params(2),
    )(sink, x, q, k, v, w_o, g_post)


def _column_blocked(w, block):
    *lead, rows, cols = w.shape
    nl = len(lead)
    w = w.reshape(*lead, rows, cols // block, block)
    return w.transpose(*range(nl), nl + 1, nl, nl + 2)


def _rope_tables(seq):
    inv_freq = 1.0 / (ROPE_THETA ** (jnp.arange(0, HEAD_DIM, 2, dtype=F32) / HEAD_DIM))
    ang = jnp.arange(seq, dtype=F32)[:, None] * inv_freq[None, :]
    cos = jnp.cos(ang)
    sin = jnp.sin(ang)
    return jnp.concatenate([cos, cos], axis=-1), jnp.concatenate([-sin, sin], axis=-1)


def _trunk(x3, w, ffn_bf16, *, tm_ffn, tf, tm_conv, tc, tm_qkv, tq):
    b, seq, d = x3.shape
    x = x3.reshape(b * seq, d)
    depth = w["ffn_w_gate"].shape[0]
    cos, sin = _rope_tables(seq)
    for i in range(depth):
        j = i // 2
        g_pre = w["g_mix_pre"][i][None]
        g_post = w["g_mix_post"][i][None]
        if i % 2 == 0:
            x = _conv_mixer(x, g_pre, w["conv_w_in"], w["conv_w_dw"], w["conv_w_out"], g_post,
                            layer=j, seq=seq, tm=tm_conv, tc=tc)
        else:
            q, k, v = _qkv_rope(x, g_pre, w["attn_w_qkv"], cos, sin, layer=j, seq=seq, tm=tm_qkv)
            x = _attention(x, q, k, v, w["attn_w_o"], w["attn_sink"][j], g_post,
                           layer=j, seq=seq, tq=tq)
        cast_next = None
        if i + 1 < depth and i + 1 not in ffn_bf16:
            cast_next = (tuple(w[name] for name in FFN_WEIGHTS), i + 1)
        x, produced = _ffn(x, w["g_ffn_pre"][i][None], *ffn_bf16[i], w["g_ffn_post"][i][None],
                           tm=tm_ffn, tf=tf, cast_next=cast_next)
        if produced is not None:
            ffn_bf16[i + 1] = produced
    return x.reshape(b, seq, d)


def kernel(x_prompt, x_sample, conv_w_in, conv_w_dw, conv_w_out, attn_w_qkv, attn_w_o, attn_sink,
           ffn_w_gate, ffn_w_up, ffn_w_down, g_mix_pre, g_mix_post, g_ffn_pre, g_ffn_post):
    tiles = dict(tm_ffn=1024, tf=512, tm_conv=1024, tc=512, tm_qkv=512, tq=512)
    w = dict(
        conv_w_in=_column_blocked(conv_w_in.astype(BF16), tiles["tc"]),
        conv_w_dw=conv_w_dw, conv_w_out=conv_w_out.astype(BF16),
        attn_w_qkv=attn_w_qkv.astype(BF16), attn_w_o=attn_w_o.astype(BF16), attn_sink=attn_sink,
        ffn_w_gate=ffn_w_gate, ffn_w_up=ffn_w_up, ffn_w_down=ffn_w_down,
        g_mix_pre=g_mix_pre, g_mix_post=g_mix_post, g_ffn_pre=g_ffn_pre, g_ffn_post=g_ffn_post,
    )
    ffn_bf16 = {0: (_column_blocked(ffn_w_gate[0].astype(BF16), tiles["tf"]),
                    _column_blocked(ffn_w_up[0].astype(BF16), tiles["tf"]),
                    ffn_w_down[0].astype(BF16))}
    y_prompt = _trunk(x_prompt, w, ffn_bf16, **tiles)
    y_sample = _trunk(x_sample, w, ffn_bf16, **tiles)
    return y_prompt, y_sample
```

```python
import functools

import jax
import jax.numpy as jnp
from jax import lax
from jax.experimental import pallas as pl
from jax.experimental.pallas import tpu as pltpu

EPS = 1e-6
NEG_INF = -1e30
LOG2E = 1.4426950408889634
ROPE_THETA = 10000.0
HEAD_DIM = 128
N_KV_HEADS = 4
WINDOW = 128
BLOCK = 128
BAND = 3 * BLOCK
CONV_WIDTH = 3
HALO = 16
BF16 = jnp.bfloat16
F32 = jnp.float32

VMEM_LIMIT_BYTES = 62 * 1024 * 1024

FFN_ROW_BLOCK = 256
CONV_ROW_BLOCK = 256
ATTN_ROW_BLOCK = 256
FFN_WEIGHTS = ("ffn_w_gate", "ffn_w_up", "ffn_w_down")


def _rms(x, g):
    r = lax.rsqrt(jnp.mean(x * x, axis=-1, keepdims=True) + EPS)
    return (x * r) * g


def _params(n_grid_dims):
    sem = ("parallel",) + ("arbitrary",) * (n_grid_dims - 1)
    return pltpu.CompilerParams(dimension_semantics=sem, vmem_limit_bytes=VMEM_LIMIT_BYTES)


def _pipelined(n_blocks, stage_a, stage_b):
    cur = stage_a(0)
    for i in range(n_blocks):
        nxt = stage_a(i + 1) if i + 1 < n_blocks else None
        stage_b(i, cur)
        cur = nxt


def _ffn_step(x_ref, gpre_ref, wg_ref, wu_ref, wd_ref, gpost_ref, o_ref, hn_ref, *, first, last,
              row_block):
    def rows(rb):
        return slice(rb * row_block, (rb + 1) * row_block)

    def gate_up(rb):
        if first:
            hn_ref[rows(rb), :] = _rms(x_ref[rows(rb), :], gpre_ref[...]).astype(BF16)
        hn = hn_ref[rows(rb), :]
        return (jnp.dot(hn, wg_ref[...], preferred_element_type=F32),
                jnp.dot(hn, wu_ref[...], preferred_element_type=F32))

    def down(rb, gate_and_up):
        gate, up = gate_and_up
        act = (jax.nn.silu(gate) * up).astype(BF16)
        part = jnp.dot(act, wd_ref[...], preferred_element_type=F32)
        if not first:
            part = o_ref[rows(rb), :] + part
        if last:
            part = x_ref[rows(rb), :] + _rms(part, gpost_ref[...])
        o_ref[rows(rb), :] = part

    _pipelined(x_ref.shape[0] // row_block, gate_up, down)


def _ffn_kernel(x_ref, gpre_ref, wg_ref, wu_ref, wd_ref, gpost_ref, *rest, row_block,
                middle_row_block, n_cast):
    cast_src, o_ref, cast_dst, hn_ref = (rest[:n_cast], rest[n_cast],
                                         rest[n_cast + 1:2 * n_cast + 1], rest[-1])
    for src, dst in zip(cast_src, cast_dst):
        dst[...] = src[...].astype(dst.dtype)

    refs = (x_ref, gpre_ref, wg_ref, wu_ref, wd_ref, gpost_ref, o_ref, hn_ref)
    f = pl.program_id(1)
    last_f = pl.num_programs(1) - 1
    for first, last, rb, cond in ((True, False, row_block, f == 0),
                                  (False, False, middle_row_block, (f > 0) & (f < last_f)),
                                  (False, True, row_block, f == last_f)):
        pl.when(cond)(functools.partial(_ffn_step, *refs, first=first, last=last, row_block=rb))


def _ffn(x, g_pre, w_gate, w_up, w_down, g_post, *, tm, tf, cast_next=None):
    n, d = x.shape
    f = w_down.shape[0]
    nf = f // tf
    assert nf >= 2, "first and last d_ff chunk are separate code paths"
    n_tiles = n // tm

    def column_block_spec(w):
        if w.ndim == 3:
            return pl.BlockSpec((None, d, tf), lambda i, j: (j, 0, 0))
        return pl.BlockSpec((d, tf), lambda i, j: (0, j))

    cast_in, cast_specs_in, cast_specs_out, cast_shapes = [], [], [], []
    if cast_next is not None:
        stacked, nxt = cast_next
        for w32 in stacked:
            _, rows, cols = w32.shape
            if cols == nf * tf:
                blk = (rows // n_tiles, tf)
                src_index = lambda i, j: (nxt, i, j)
                dst_spec = pl.BlockSpec((None,) + blk, lambda i, j: (j, i, 0))
                dst_shape = (nf, rows, tf)
            else:
                blk = (rows // (n_tiles * nf), cols)
                src_index = lambda i, j: (nxt, i * nf + j, 0)
                dst_spec = pl.BlockSpec(blk, lambda i, j: (i * nf + j, 0))
                dst_shape = (rows, cols)
            assert blk[0] % HALO == 0 and rows % blk[0] == 0, "bf16 tiles are HALO rows"
            cast_in.append(w32)
            cast_specs_in.append(pl.BlockSpec((None,) + blk, src_index))
            cast_specs_out.append(dst_spec)
            cast_shapes.append(jax.ShapeDtypeStruct(dst_shape, BF16))
    outs = pl.pallas_call(
        functools.partial(_ffn_kernel, row_block=min(FFN_ROW_BLOCK, tm), middle_row_block=tm,
                          n_cast=len(cast_in)),
        name="ffn",
        grid=(n // tm, nf),
        in_specs=[
            pl.BlockSpec((tm, d), lambda i, j: (i, 0)),
            pl.BlockSpec((1, d), lambda i, j: (0, 0)),
            column_block_spec(w_gate),
            column_block_spec(w_up),
            pl.BlockSpec((tf, d), lambda i, j: (j, 0)),
            pl.BlockSpec((1, d), lambda i, j: (0, 0)),
        ] + cast_specs_in,
        out_specs=[pl.BlockSpec((tm, d), lambda i, j: (i, 0))] + cast_specs_out,
        out_shape=[jax.ShapeDtypeStruct((n, d), F32)] + cast_shapes,
        scratch_shapes=[pltpu.VMEM((tm, d), BF16)],
        compiler_params=_params(2),
    )(x, g_pre, w_gate, w_up, w_down, g_post, *cast_in)
    return outs[0], (tuple(outs[1:]) if cast_next is not None else None)


def _conv_step(xm_ref, xp_ref, xn_ref, gpre_ref, wb_ref, wc_ref, wh_ref, wdw_ref, wo_ref,
               gpost_ref, o_ref, hn_ref, *, first, last, row_block, tiles_per_seq):
    tm = xm_ref.shape[0]
    nb = tm // row_block
    w = wdw_ref[...]

    def ext_rows(rb):
        lo = HALO + rb * row_block - (HALO if rb == 0 else 0)
        hi = HALO + (rb + 1) * row_block + (HALO if rb == nb - 1 else 0)
        return lo, hi

    def main_rows(rb):
        return slice(rb * row_block, (rb + 1) * row_block)

    def projections(rb):
        lo, hi = ext_rows(rb)
        if first:
            g = gpre_ref[...]
            hn_ref[HALO + rb * row_block:HALO + (rb + 1) * row_block, :] = (
                _rms(xm_ref[main_rows(rb), :], g).astype(BF16))
            t = pl.program_id(0) % tiles_per_seq
            if rb == 0:
                hn_ref[0:HALO, :] = jnp.where(t != 0, _rms(xp_ref[...], g), 0.0).astype(BF16)
            if rb == nb - 1:
                hn_ref[HALO + tm:, :] = jnp.where(t != tiles_per_seq - 1, _rms(xn_ref[...], g),
                                                  0.0).astype(BF16)
        hn = hn_ref[lo:hi, :]
        u = (jnp.dot(hn, wc_ref[...], preferred_element_type=F32)
             * jnp.dot(hn, wh_ref[...], preferred_element_type=F32))
        b = jnp.dot(hn_ref[HALO + rb * row_block:HALO + (rb + 1) * row_block, :], wb_ref[...],
                    preferred_element_type=F32)
        return u, b

    def mix_and_project(rb, u_blocks, b):
        u = u_blocks[rb]
        n_ext = u.shape[0]
        off = HALO if rb == 0 else 0
        u_prev = pltpu.roll(u, 1, 0)[off:off + row_block]
        u_next = pltpu.roll(u, n_ext - 1, 0)[off:off + row_block]
        row = lax.broadcasted_iota(jnp.int32, (row_block, 1), 0)
        if rb > 0:
            u_prev = jnp.where(row == 0, u_blocks[rb - 1][-1:], u_prev)
        if rb < nb - 1:
            u_next = jnp.where(row == row_block - 1, u_blocks[rb + 1][0:1], u_next)
        v = w[0:1] * u_prev + w[1:2] * u[off:off + row_block] + w[2:3] * u_next
        y = (b * v).astype(BF16)
        part = jnp.dot(y, wo_ref[...], preferred_element_type=F32)
        if not first:
            part = o_ref[main_rows(rb), :] + part
        if last:
            part = xm_ref[main_rows(rb), :] + _rms(part, gpost_ref[...])
        o_ref[main_rows(rb), :] = part

    u_blocks, b_blocks = {}, {}
    u_blocks[0], b_blocks[0] = projections(0)
    for rb in range(nb):
        if rb + 1 < nb:
            u_blocks[rb + 1], b_blocks[rb + 1] = projections(rb + 1)
        mix_and_project(rb, u_blocks, b_blocks[rb])


def _conv_kernel(*refs, row_block, tiles_per_seq):
    j = pl.program_id(1)
    last_j = pl.num_programs(1) - 1
    for first, last, cond in ((True, False, j == 0),
                              (False, False, (j > 0) & (j < last_j)),
                              (False, True, j == last_j)):
        pl.when(cond)(functools.partial(_conv_step, *refs, first=first, last=last,
                                        row_block=row_block, tiles_per_seq=tiles_per_seq))


def _conv_mixer(x, g_pre, w_in, w_dw, w_out, g_post, *, layer, seq, tm, tc):
    n, d = x.shape
    nc = d // tc
    assert nc >= 2, "first and last channel chunk are separate code paths"
    hb = tm // HALO
    n_hb = n // HALO
    kern = functools.partial(_conv_kernel, row_block=min(CONV_ROW_BLOCK, tm),
                             tiles_per_seq=seq // tm)
    return pl.pallas_call(
        kern,
        name="conv_mixer",
        grid=(n // tm, nc),
        in_specs=[
            pl.BlockSpec((tm, d), lambda i, j: (i, 0)),
            pl.BlockSpec((HALO, d), lambda i, j: (jnp.maximum(i * hb - 1, 0), 0)),
            pl.BlockSpec((HALO, d), lambda i, j: (jnp.minimum((i + 1) * hb, n_hb - 1), 0)),
            pl.BlockSpec((1, d), lambda i, j: (0, 0)),
            pl.BlockSpec((None, d, tc), lambda i, j: (layer, 0, j)),
            pl.BlockSpec((None, d, tc), lambda i, j: (layer, 0, nc + j)),
            pl.BlockSpec((None, d, tc), lambda i, j: (layer, 0, 2 * nc + j)),
            pl.BlockSpec((None, CONV_WIDTH, tc), lambda i, j: (layer, 0, j)),
            pl.BlockSpec((None, tc, d), lambda i, j: (layer, j, 0)),
            pl.BlockSpec((1, d), lambda i, j: (0, 0)),
        ],
        out_specs=pl.BlockSpec((tm, d), lambda i, j: (i, 0)),
        out_shape=jax.ShapeDtypeStruct((n, d), F32),
        scratch_shapes=[pltpu.VMEM((tm + 2 * HALO, d), BF16)],
        compiler_params=_params(2),
    )(x, x, x, g_pre, w_in, w_in, w_in, w_dw, w_out, g_post)


def _qkv_kernel(x_ref, gpre_ref, w_ref, cos_ref, sin_ref, q_ref, k_ref, v_ref):
    dq = q_ref.shape[1]
    dk = k_ref.shape[1]
    hn = _rms(x_ref[...], gpre_ref[...]).astype(BF16)
    qkv = jnp.dot(hn, w_ref[...], preferred_element_type=F32)
    cos = cos_ref[...]
    sin = sin_ref[...]

    def rope(xh):
        return xh * cos + pltpu.roll(xh, HEAD_DIM // 2, 1) * sin

    for hd in range(dq // HEAD_DIM):
        sl = slice(hd * HEAD_DIM, (hd + 1) * HEAD_DIM)
        q_ref[:, sl] = rope(qkv[:, sl]).astype(BF16)
    for hd in range(dk // HEAD_DIM):
        sl = slice(hd * HEAD_DIM, (hd + 1) * HEAD_DIM)
        k_ref[:, sl] = rope(qkv[:, dq + hd * HEAD_DIM:dq + (hd + 1) * HEAD_DIM]).astype(BF16)
    v_ref[...] = qkv[:, dq + dk:].astype(BF16)


def _qkv_rope(x, g_pre, w_qkv, cos, sin, *, layer, seq, tm):
    n, d = x.shape
    dqkv = w_qkv.shape[2]
    dk = N_KV_HEADS * HEAD_DIM
    dq = dqkv - 2 * dk
    tps = seq // tm
    return pl.pallas_call(
        _qkv_kernel,
        name="qkv_rope",
        grid=(n // tm,),
        in_specs=[
            pl.BlockSpec((tm, d), lambda i: (i, 0)),
            pl.BlockSpec((1, d), lambda i: (0, 0)),
            pl.BlockSpec((None, d, dqkv), lambda i: (layer, 0, 0), pipeline_mode=pl.Buffered(1)),
            pl.BlockSpec((tm, HEAD_DIM), lambda i: (i % tps, 0)),
            pl.BlockSpec((tm, HEAD_DIM), lambda i: (i % tps, 0)),
        ],
        out_specs=[
            pl.BlockSpec((tm, dq), lambda i: (i, 0)),
            pl.BlockSpec((tm, dk), lambda i: (i, 0)),
            pl.BlockSpec((tm, dk), lambda i: (i, 0)),
        ],
        out_shape=[
            jax.ShapeDtypeStruct((n, dq), BF16),
            jax.ShapeDtypeStruct((n, dk), BF16),
            jax.ShapeDtypeStruct((n, dk), BF16),
        ],
        compiler_params=_params(1),
    )(x, g_pre, w_qkv, cos, sin)


def _attn_kernel(sink_ref, x_ref, q_ref, k_ref, v_ref, wo_ref, gpost_ref, o_ref, att_ref, *, seq):
    qi = pl.program_id(1)
    tq = q_ref.shape[0]
    n_heads = q_ref.shape[1] // HEAD_DIM
    group = n_heads // N_KV_HEADS
    exp_coef = HEAD_DIM ** -0.5 * LOG2E
    inv_scale = HEAD_DIM ** 0.5
    qrow_minus_krow = (lax.broadcasted_iota(jnp.int32, (BAND, BLOCK), 1)
                       - lax.broadcasted_iota(jnp.int32, (BAND, BLOCK), 0))
    one_hot_rows = (lax.broadcasted_iota(jnp.int32, (BLOCK, BLOCK), 0)
                    == lax.broadcasted_iota(jnp.int32, (BLOCK, BLOCK), 1))
    one_hot_rows = jnp.where(one_hot_rows, 1.0, 0.0).astype(BF16)
    ones_cols = jnp.ones((BAND, HEAD_DIM), BF16)
    sink_row = ((lax.broadcasted_iota(jnp.int32, (BLOCK, 2 * HEAD_DIM), 0) == 0)
                & (lax.broadcasted_iota(jnp.int32, (BLOCK, 2 * HEAD_DIM), 1) >= HEAD_DIM))
    sink_rows = jnp.where(sink_row, 1.0, 0.0).astype(BF16)

    def window(qb):
        qpos0 = qi * tq + qb * BLOCK
        win0 = pl.multiple_of(jnp.clip(qpos0 - BLOCK, 0, seq - BAND), BLOCK)
        diff = qrow_minus_krow + (qpos0 - win0)
        valid = (diff <= WINDOW) & (diff >= -WINDOW)
        return win0, jnp.where(valid, 0.0, NEG_INF).astype(BF16)

    def scores(qb, kv, win0, mask):
        rows = slice(qb * BLOCK, (qb + 1) * BLOCK)
        kw = k_ref[pl.ds(win0, BAND), kv * HEAD_DIM:(kv + 1) * HEAD_DIM]
        k_ext = jnp.concatenate([kw, mask], axis=1)
        q_ext = jnp.concatenate(
            [jnp.concatenate(
                [q_ref[rows, (kv * group + g) * HEAD_DIM:(kv * group + g + 1) * HEAD_DIM],
                 one_hot_rows], axis=1)
             for g in range(group)], axis=0)
        return lax.dot_general(q_ext, k_ext, (((1,), (1,)), ((), ())), preferred_element_type=F32)

    def probabilities(kv, s):
        s = s.reshape(group, BLOCK, BAND)
        sink = jnp.concatenate(
            [jnp.full((1, BLOCK, BLOCK), sink_ref[kv * group + g] * inv_scale, F32)
             for g in range(group)], axis=0)
        s = jnp.concatenate([s, sink], axis=-1)
        m = jnp.max(s, axis=-1, keepdims=True)
        return jnp.exp2((s - m) * exp_coef).astype(BF16).reshape(group * BLOCK, BAND + BLOCK)

    def weighted_values(qb, kv, p, win0):
        rows = slice(qb * BLOCK, (qb + 1) * BLOCK)
        vw = v_ref[pl.ds(win0, BAND), kv * HEAD_DIM:(kv + 1) * HEAD_DIM]
        v_ext = jnp.concatenate([jnp.concatenate([vw, ones_cols], axis=1), sink_rows], axis=0)
        r = jnp.dot(p, v_ext, preferred_element_type=F32)
        og = r[:, :HEAD_DIM] / r[:, HEAD_DIM:]
        for g in range(group):
            hd = kv * group + g
            att_ref[rows, hd * HEAD_DIM:(hd + 1) * HEAD_DIM] = og[g * BLOCK:(g + 1) * BLOCK].astype(BF16)

    row_block = min(ATTN_ROW_BLOCK, tq)

    def project_rows(rb):
        rows = slice(rb * row_block, (rb + 1) * row_block)
        y = jnp.dot(att_ref[rows, :], wo_ref[...], preferred_element_type=F32)
        o_ref[rows, :] = x_ref[rows, :] + _rms(y, gpost_ref[...])

    steps = [(qb, kv) for qb in range(tq // BLOCK) for kv in range(N_KV_HEADS)]
    win = {qb: window(qb) for qb in range(tq // BLOCK)}
    steps_per_row_block = (row_block // BLOCK) * N_KV_HEADS
    s_cur = scores(*steps[0], *win[steps[0][0]])
    for idx, (qb, kv) in enumerate(steps):
        if idx + 1 < len(steps):
            nqb, nkv = steps[idx + 1]
            s_next = scores(nqb, nkv, *win[nqb])
        weighted_values(qb, kv, probabilities(kv, s_cur), win[qb][0])
        s_cur = s_next
        if (idx + 1) % steps_per_row_block == 0:
            project_rows(idx // steps_per_row_block)


def _attention(x, q, k, v, w_o, sink, g_post, *, layer, seq, tq):
    n, d = x.shape
    dq = q.shape[1]
    dk = k.shape[1]
    tps = seq // tq
    kern = functools.partial(_attn_kernel, seq=seq)
    grid_spec = pltpu.PrefetchScalarGridSpec(
        num_scalar_prefetch=1,
        grid=(n // seq, tps),
        in_specs=[
            pl.BlockSpec((tq, d), lambda b, i, s: (b * tps + i, 0)),
            pl.BlockSpec((tq, dq), lambda b, i, s: (b * tps + i, 0)),
            pl.BlockSpec((seq, dk), lambda b, i, s: (b, 0)),
            pl.BlockSpec((seq, dk), lambda b, i, s: (b, 0)),
            pl.BlockSpec((None, dq, d), lambda b, i, s: (layer, 0, 0), pipeline_mode=pl.Buffered(1)),
            pl.BlockSpec((1, d), lambda b, i, s: (0, 0)),
        ],
        out_specs=pl.BlockSpec((tq, d), lambda b, i, s: (b * tps + i, 0)),
        scratch_shapes=[pltpu.VMEM((tq, dq), BF16)],
    )
    return pl.pallas_call(
        kern,
        name="attention",
        grid_spec=grid_spec,
        out_shape=jax.ShapeDtypeStruct((n, d), F32),
        compiler_params=_params(2),
    )(sink, x, q, k, v, w_o, g_post)


def _rope_tables(seq):
    inv_freq = 1.0 / (ROPE_THETA ** (jnp.arange(0, HEAD_DIM, 2, dtype=F32) / HEAD_DIM))
    ang = jnp.arange(seq, dtype=F32)[:, None] * inv_freq[None, :]
    cos = jnp.cos(ang)
    sin = jnp.sin(ang)
    return jnp.concatenate([cos, cos], axis=-1), jnp.concatenate([-sin, sin], axis=-1)


def _trunk(x3, w, ffn_bf16, *, tm_ffn, tf, tm_conv, tc, tm_qkv, tq):
    b, seq, d = x3.shape
    x = x3.reshape(b * seq, d)
    depth = w["ffn_w_gate"].shape[0]
    cos, sin = _rope_tables(seq)
    for i in range(depth):
        j = i // 2
        g_pre = w["g_mix_pre"][i][None]
        g_post = w["g_mix_post"][i][None]
        if i % 2 == 0:
            x = _conv_mixer(x, g_pre, w["conv_w_in"], w["conv_w_dw"], w["conv_w_out"], g_post,
                            layer=j, seq=seq, tm=tm_conv, tc=tc)
        else:
            q, k, v = _qkv_rope(x, g_pre, w["attn_w_qkv"], cos, sin, layer=j, seq=seq, tm=tm_qkv)
            x = _attention(x, q, k, v, w["attn_w_o"], w["attn_sink"][j], g_post,
                           layer=j, seq=seq, tq=tq)
        cast_next = None
        if i + 1 < depth and i + 1 not in ffn_bf16:
            cast_next = (tuple(w[name] for name in FFN_WEIGHTS), i + 1)
        x, produced = _ffn(x, w["g_ffn_pre"][i][None], *ffn_bf16[i], w["g_ffn_post"][i][None],
                           tm=tm_ffn, tf=tf, cast_next=cast_next)
        if produced is not None:
            ffn_bf16[i + 1] = produced
    return x.reshape(b, seq, d)


def kernel(x_prompt, x_sample, conv_w_in, conv_w_dw, conv_w_out, attn_w_qkv, attn_w_o, attn_sink,
           ffn_w_gate, ffn_w_up, ffn_w_down, g_mix_pre, g_mix_post, g_ffn_pre, g_ffn_post):
    w = dict(
        conv_w_in=conv_w_in.astype(BF16), conv_w_dw=conv_w_dw, conv_w_out=conv_w_out.astype(BF16),
        attn_w_qkv=attn_w_qkv.astype(BF16), attn_w_o=attn_w_o.astype(BF16), attn_sink=attn_sink,
        ffn_w_gate=ffn_w_gate, ffn_w_up=ffn_w_up, ffn_w_down=ffn_w_down,
        g_mix_pre=g_mix_pre, g_mix_post=g_mix_post, g_ffn_pre=g_ffn_pre, g_ffn_post=g_ffn_post,
    )
    ffn_bf16 = {0: tuple(w[name][0].astype(BF16) for name in FFN_WEIGHTS)}
    tiles = dict(tm_ffn=1024, tf=512, tm_conv=1024, tc=512, tm_qkv=512, tq=512)
    y_prompt = _trunk(x_prompt, w, ffn_bf16, **tiles)
    y_sample = _trunk(x_sample, w, ffn_bf16, **tiles)
    return y_prompt, y_sample
```

```python
import functools

import jax
import jax.numpy as jnp
from jax import lax
from jax.experimental import pallas as pl
from jax.experimental.pallas import tpu as pltpu

EPS = 1e-6
NEG_INF = -1e30
LOG2E = 1.4426950408889634
ROPE_THETA = 10000.0
HEAD_DIM = 128
N_KV_HEADS = 4
WINDOW = 128
BLOCK = 128
BAND = 3 * BLOCK
CONV_WIDTH = 3
HALO = 16
LANES = 128
SINK_ROWS = 16

BF16 = jnp.bfloat16
F32 = jnp.float32

VMEM_LIMIT_BYTES = 62 * 1024 * 1024

FFN_ROW_BLOCK = 256
CONV_ROW_BLOCK = 256
ATTN_ROW_BLOCK = 256
FFN_WEIGHTS = ("ffn_w_gate", "ffn_w_up", "ffn_w_down")


def _rms(x, g):
    r = lax.rsqrt(jnp.mean(x * x, axis=-1, keepdims=True) + EPS)
    return (x * r) * g


def _params(n_grid_dims):
    sem = ("parallel",) + ("arbitrary",) * (n_grid_dims - 1)
    return pltpu.CompilerParams(dimension_semantics=sem, vmem_limit_bytes=VMEM_LIMIT_BYTES)


def _pipelined(n_blocks, stage_a, stage_b):
    cur = stage_a(0)
    for i in range(n_blocks):
        nxt = stage_a(i + 1) if i + 1 < n_blocks else None
        stage_b(i, cur)
        cur = nxt


def _ffn_step(x_ref, gpre_ref, wg_ref, wu_ref, wd_ref, gpost_ref, o_ref, hn_ref, *, first, last,
              row_block):
    def rows(rb):
        return slice(rb * row_block, (rb + 1) * row_block)

    def gate_up(rb):
        if first:
            hn_ref[rows(rb), :] = _rms(x_ref[rows(rb), :], gpre_ref[...]).astype(BF16)
        hn = hn_ref[rows(rb), :]
        return (jnp.dot(hn, wg_ref[...], preferred_element_type=F32),
                jnp.dot(hn, wu_ref[...], preferred_element_type=F32))

    def down(rb, gate_and_up):
        gate, up = gate_and_up
        act = (jax.nn.silu(gate) * up).astype(BF16)
        part = jnp.dot(act, wd_ref[...], preferred_element_type=F32)
        if not first:
            part = o_ref[rows(rb), :] + part
        if last:
            part = x_ref[rows(rb), :] + _rms(part, gpost_ref[...])
        o_ref[rows(rb), :] = part

    _pipelined(x_ref.shape[0] // row_block, gate_up, down)


def _ffn_kernel(x_ref, gpre_ref, wg_ref, wu_ref, wd_ref, gpost_ref, *rest, row_block,
                middle_row_block, n_cast):
    cast_src, o_ref, cast_dst, hn_ref = (rest[:n_cast], rest[n_cast],
                                         rest[n_cast + 1:2 * n_cast + 1], rest[-1])
    for src, dst in zip(cast_src, cast_dst):
        dst[...] = src[...].astype(dst.dtype)

    refs = (x_ref, gpre_ref, wg_ref, wu_ref, wd_ref, gpost_ref, o_ref, hn_ref)
    f = pl.program_id(1)
    last_f = pl.num_programs(1) - 1
    for first, last, rb, cond in ((True, False, row_block, f == 0),
                                  (False, False, middle_row_block, (f > 0) & (f < last_f)),
                                  (False, True, row_block, f == last_f)):
        pl.when(cond)(functools.partial(_ffn_step, *refs, first=first, last=last, row_block=rb))


def _ffn(x, g_pre, w_gate, w_up, w_down, g_post, *, tm, tf, cast_next=None):
    n, d = x.shape
    f = w_gate.shape[1]
    nf = f // tf
    assert nf >= 2, "first and last d_ff chunk are separate code paths"
    n_tiles = n // tm
    cast_in, cast_specs_in, cast_specs_out, cast_shapes = [], [], [], []
    if cast_next is not None:
        stacked, nxt = cast_next
        for w32 in stacked:
            _, rows, cols = w32.shape
            if cols % (nf * LANES) == 0:
                blk, index = (rows // n_tiles, cols // nf), (lambda i, j: (i, j))
            else:
                blk, index = (rows // (n_tiles * nf), cols), (lambda i, j: (i * nf + j, 0))
            assert blk[0] % HALO == 0 and rows % blk[0] == 0, "bf16 tiles are HALO rows"
            cast_in.append(w32)
            cast_specs_in.append(pl.BlockSpec(
                (None,) + blk, lambda i, j, index=index: (nxt,) + index(i, j)))
            cast_specs_out.append(pl.BlockSpec(blk, index))
            cast_shapes.append(jax.ShapeDtypeStruct((rows, cols), BF16))
    outs = pl.pallas_call(
        functools.partial(_ffn_kernel, row_block=min(FFN_ROW_BLOCK, tm), middle_row_block=tm,
                          n_cast=len(cast_in)),
        name="ffn",
        grid=(n // tm, nf),
        in_specs=[
            pl.BlockSpec((tm, d), lambda i, j: (i, 0)),
            pl.BlockSpec((1, d), lambda i, j: (0, 0)),
            pl.BlockSpec((d, tf), lambda i, j: (0, j)),
            pl.BlockSpec((d, tf), lambda i, j: (0, j)),
            pl.BlockSpec((tf, d), lambda i, j: (j, 0)),
            pl.BlockSpec((1, d), lambda i, j: (0, 0)),
        ] + cast_specs_in,
        out_specs=[pl.BlockSpec((tm, d), lambda i, j: (i, 0))] + cast_specs_out,
        out_shape=[jax.ShapeDtypeStruct((n, d), F32)] + cast_shapes,
        scratch_shapes=[pltpu.VMEM((tm, d), BF16)],
        compiler_params=_params(2),
    )(x, g_pre, w_gate, w_up, w_down, g_post, *cast_in)
    return outs[0], (tuple(outs[1:]) if cast_next is not None else None)


def _conv_step(xm_ref, xp_ref, xn_ref, gpre_ref, wb_ref, wc_ref, wh_ref, wdw_ref, wo_ref,
               gpost_ref, o_ref, hn_ref, *, first, last, row_block, tiles_per_seq):
    tm = xm_ref.shape[0]
    nb = tm // row_block
    w = wdw_ref[...]

    def ext_rows(rb):
        lo = HALO + rb * row_block - (HALO if rb == 0 else 0)
        hi = HALO + (rb + 1) * row_block + (HALO if rb == nb - 1 else 0)
        return lo, hi

    def main_rows(rb):
        return slice(rb * row_block, (rb + 1) * row_block)

    def projections(rb):
        lo, hi = ext_rows(rb)
        if first:
            g = gpre_ref[...]
            hn_ref[HALO + rb * row_block:HALO + (rb + 1) * row_block, :] = (
                _rms(xm_ref[main_rows(rb), :], g).astype(BF16))
            t = pl.program_id(0) % tiles_per_seq
            if rb == 0:
                hn_ref[0:HALO, :] = jnp.where(t != 0, _rms(xp_ref[...], g), 0.0).astype(BF16)
            if rb == nb - 1:
                hn_ref[HALO + tm:, :] = jnp.where(t != tiles_per_seq - 1, _rms(xn_ref[...], g),
                                                  0.0).astype(BF16)
        hn = hn_ref[lo:hi, :]
        u = (jnp.dot(hn, wc_ref[...], preferred_element_type=F32)
             * jnp.dot(hn, wh_ref[...], preferred_element_type=F32))
        b = jnp.dot(hn_ref[HALO + rb * row_block:HALO + (rb + 1) * row_block, :], wb_ref[...],
                    preferred_element_type=F32)
        return u, b

    def mix_and_project(rb, u_blocks, b):
        u = u_blocks[rb]
        n_ext = u.shape[0]
        off = HALO if rb == 0 else 0
        u_prev = pltpu.roll(u, 1, 0)[off:off + row_block]
        u_next = pltpu.roll(u, n_ext - 1, 0)[off:off + row_block]
        row = lax.broadcasted_iota(jnp.int32, (row_block, 1), 0)
        if rb > 0:
            u_prev = jnp.where(row == 0, u_blocks[rb - 1][-1:], u_prev)
        if rb < nb - 1:
            u_next = jnp.where(row == row_block - 1, u_blocks[rb + 1][0:1], u_next)
        v = w[0:1] * u_prev + w[1:2] * u[off:off + row_block] + w[2:3] * u_next
        y = (b * v).astype(BF16)
        part = jnp.dot(y, wo_ref[...], preferred_element_type=F32)
        if not first:
            part = o_ref[main_rows(rb), :] + part
        if last:
            part = xm_ref[main_rows(rb), :] + _rms(part, gpost_ref[...])
        o_ref[main_rows(rb), :] = part

    u_blocks, b_blocks = {}, {}
    u_blocks[0], b_blocks[0] = projections(0)
    for rb in range(nb):
        if rb + 1 < nb:
            u_blocks[rb + 1], b_blocks[rb + 1] = projections(rb + 1)
        mix_and_project(rb, u_blocks, b_blocks[rb])


def _conv_kernel(*refs, row_block, tiles_per_seq):
    j = pl.program_id(1)
    last_j = pl.num_programs(1) - 1
    for first, last, cond in ((True, False, j == 0),
                              (False, False, (j > 0) & (j < last_j)),
                              (False, True, j == last_j)):
        pl.when(cond)(functools.partial(_conv_step, *refs, first=first, last=last,
                                        row_block=row_block, tiles_per_seq=tiles_per_seq))


def _conv_mixer(x, g_pre, w_in, w_dw, w_out, g_post, *, layer, seq, tm, tc):
    n, d = x.shape
    nc = d // tc
    assert nc >= 2, "first and last channel chunk are separate code paths"
    hb = tm // HALO
    n_hb = n // HALO
    kern = functools.partial(_conv_kernel, row_block=min(CONV_ROW_BLOCK, tm),
                             tiles_per_seq=seq // tm)
    return pl.pallas_call(
        kern,
        name="conv_mixer",
        grid=(n // tm, nc),
        in_specs=[
            pl.BlockSpec((tm, d), lambda i, j: (i, 0)),
            pl.BlockSpec((HALO, d), lambda i, j: (jnp.maximum(i * hb - 1, 0), 0)),
            pl.BlockSpec((HALO, d), lambda i, j: (jnp.minimum((i + 1) * hb, n_hb - 1), 0)),
            pl.BlockSpec((1, d), lambda i, j: (0, 0)),
            pl.BlockSpec((None, d, tc), lambda i, j: (layer, 0, j)),
            pl.BlockSpec((None, d, tc), lambda i, j: (layer, 0, nc + j)),
            pl.BlockSpec((None, d, tc), lambda i, j: (layer, 0, 2 * nc + j)),
            pl.BlockSpec((None, CONV_WIDTH, tc), lambda i, j: (layer, 0, j)),
            pl.BlockSpec((None, tc, d), lambda i, j: (layer, j, 0)),
            pl.BlockSpec((1, d), lambda i, j: (0, 0)),
        ],
        out_specs=pl.BlockSpec((tm, d), lambda i, j: (i, 0)),
        out_shape=jax.ShapeDtypeStruct((n, d), F32),
        scratch_shapes=[pltpu.VMEM((tm + 2 * HALO, d), BF16)],
        compiler_params=_params(2),
    )(x, x, x, g_pre, w_in, w_in, w_in, w_dw, w_out, g_post)


def _qkv_kernel(x_ref, gpre_ref, w_ref, cos_ref, sin_ref, q_ref, k_ref, v_ref):
    dq = q_ref.shape[1]
    dk = k_ref.shape[1]
    hn = _rms(x_ref[...], gpre_ref[...]).astype(BF16)
    qkv = jnp.dot(hn, w_ref[...], preferred_element_type=F32)
    cos = cos_ref[...]
    sin = sin_ref[...]

    def rope(xh):
        return xh * cos + pltpu.roll(xh, HEAD_DIM // 2, 1) * sin

    for hd in range(dq // HEAD_DIM):
        sl = slice(hd * HEAD_DIM, (hd + 1) * HEAD_DIM)
        q_ref[:, sl] = rope(qkv[:, sl]).astype(BF16)
    for hd in range(dk // HEAD_DIM):
        sl = slice(hd * HEAD_DIM, (hd + 1) * HEAD_DIM)
        k_ref[:, sl] = rope(qkv[:, dq + hd * HEAD_DIM:dq + (hd + 1) * HEAD_DIM]).astype(BF16)
    for blk in range(x_ref.shape[0] // BLOCK):
        v_ref[blk] = qkv[blk * BLOCK:(blk + 1) * BLOCK, dq + dk:].T.astype(BF16)


def _qkv_rope(x, g_pre, w_qkv, cos, sin, *, layer, seq, tm):
    n, d = x.shape
    dqkv = w_qkv.shape[2]
    dk = N_KV_HEADS * HEAD_DIM
    dq = dqkv - 2 * dk
    tps = seq // tm
    return pl.pallas_call(
        _qkv_kernel,
        name="qkv_rope",
        grid=(n // tm,),
        in_specs=[
            pl.BlockSpec((tm, d), lambda i: (i, 0)),
            pl.BlockSpec((1, d), lambda i: (0, 0)),
            pl.BlockSpec((None, d, dqkv), lambda i: (layer, 0, 0), pipeline_mode=pl.Buffered(1)),
            pl.BlockSpec((tm, HEAD_DIM), lambda i: (i % tps, 0)),
            pl.BlockSpec((tm, HEAD_DIM), lambda i: (i % tps, 0)),
        ],
        out_specs=[
            pl.BlockSpec((tm, dq), lambda i: (i, 0)),
            pl.BlockSpec((tm, dk), lambda i: (i, 0)),
            pl.BlockSpec((tm // BLOCK, dk, BLOCK), lambda i: (i, 0, 0)),
        ],
        out_shape=[
            jax.ShapeDtypeStruct((n, dq), BF16),
            jax.ShapeDtypeStruct((n, dk), BF16),
            jax.ShapeDtypeStruct((n // BLOCK, dk, BLOCK), BF16),
        ],
        compiler_params=_params(1),
    )(x, g_pre, w_qkv, cos, sin)


def _attn_kernel(sink_ref, x_ref, q_ref, k_ref, v_ref, wo_ref, gpost_ref, o_ref, att_ref, *, seq):
    qi = pl.program_id(1)
    tq = q_ref.shape[0]
    n_heads = q_ref.shape[1] // HEAD_DIM
    group = n_heads // N_KV_HEADS
    exp_coef = HEAD_DIM ** -0.5 * LOG2E
    inv_scale = HEAD_DIM ** 0.5
    qrow_minus_krow = (lax.broadcasted_iota(jnp.int32, (BAND, BLOCK), 1)
                       - lax.broadcasted_iota(jnp.int32, (BAND, BLOCK), 0))
    one_hot_rows = (lax.broadcasted_iota(jnp.int32, (BLOCK, BLOCK), 0)
                    == lax.broadcasted_iota(jnp.int32, (BLOCK, BLOCK), 1))
    one_hot_rows = jnp.where(one_hot_rows, 1.0, 0.0).astype(BF16)
    lanes = group * BLOCK
    den_row = ((lax.broadcasted_iota(jnp.int32, (SINK_ROWS, BAND + BLOCK), 0) == 0)
               & (lax.broadcasted_iota(jnp.int32, (SINK_ROWS, BAND + BLOCK), 1) <= BAND))
    den_rows = jnp.where(den_row, 1.0, 0.0).astype(BF16)
    zero_values = jnp.zeros((HEAD_DIM, BLOCK), BF16)
    zero_keys = jnp.zeros((BLOCK - SINK_ROWS, lanes), BF16)
    blocks_per_seq = seq // BLOCK

    def window(qb):
        qblk = qi * (tq // BLOCK) + qb
        wblk = jnp.clip(qblk - 1, 0, blocks_per_seq - BAND // BLOCK)
        diff = qrow_minus_krow + (qblk - wblk) * BLOCK
        valid = (diff <= WINDOW) & (diff >= -WINDOW)
        return wblk, jnp.where(valid, 0.0, NEG_INF).astype(BF16)

    def scores(qb, kv, wblk, mask):
        rows = slice(qb * BLOCK, (qb + 1) * BLOCK)
        win0 = pl.multiple_of(wblk * BLOCK, BLOCK)
        kw = k_ref[pl.ds(win0, BAND), kv * HEAD_DIM:(kv + 1) * HEAD_DIM]
        k_ext = jnp.concatenate([kw, mask], axis=1)
        q_ext = jnp.concatenate(
            [jnp.concatenate(
                [q_ref[rows, (kv * group + g) * HEAD_DIM:(kv * group + g + 1) * HEAD_DIM],
                 one_hot_rows], axis=1)
             for g in range(group)], axis=0)
        return lax.dot_general(k_ext, q_ext, (((1,), (1,)), ((), ())), preferred_element_type=F32)

    def probabilities(kv, s):
        sink = jnp.concatenate(
            [jnp.full((1, BLOCK), sink_ref[kv * group + g] * inv_scale, F32)
             for g in range(group)], axis=1)
        m = jnp.maximum(jnp.max(s, axis=0, keepdims=True), sink)
        p = jnp.exp2((s - m) * exp_coef).astype(BF16)
        p_sink = jnp.broadcast_to(jnp.exp2((sink - m) * exp_coef), (SINK_ROWS, lanes)).astype(BF16)
        return jnp.concatenate([p, p_sink, zero_keys], axis=0)

    def weighted_values(qb, kv, p, wblk):
        rows = slice(qb * BLOCK, (qb + 1) * BLOCK)
        vt = v_ref[pl.ds(wblk, BAND // BLOCK), kv * HEAD_DIM:(kv + 1) * HEAD_DIM, :]
        v_ext = jnp.concatenate(
            [jnp.concatenate([vt[t] for t in range(BAND // BLOCK)] + [zero_values], axis=1),
             den_rows], axis=0)
        r = jnp.dot(v_ext, p, preferred_element_type=F32)
        og = r[:HEAD_DIM] / r[HEAD_DIM:HEAD_DIM + 1]
        for g in range(group):
            hd = kv * group + g
            att_ref[rows, hd * HEAD_DIM:(hd + 1) * HEAD_DIM] = (
                og[:, g * BLOCK:(g + 1) * BLOCK].T.astype(BF16))

    row_block = min(ATTN_ROW_BLOCK, tq)

    def project_rows(rb):
        rows = slice(rb * row_block, (rb + 1) * row_block)
        y = jnp.dot(att_ref[rows, :], wo_ref[...], preferred_element_type=F32)
        o_ref[rows, :] = x_ref[rows, :] + _rms(y, gpost_ref[...])

    steps = [(qb, kv) for qb in range(tq // BLOCK) for kv in range(N_KV_HEADS)]
    win = {qb: window(qb) for qb in range(tq // BLOCK)}
    steps_per_row_block = (row_block // BLOCK) * N_KV_HEADS
    s_cur = scores(*steps[0], *win[steps[0][0]])
    for idx, (qb, kv) in enumerate(steps):
        if idx + 1 < len(steps):
            nqb, nkv = steps[idx + 1]
            s_next = scores(nqb, nkv, *win[nqb])
        weighted_values(qb, kv, probabilities(kv, s_cur), win[qb][0])
        s_cur = s_next
        if (idx + 1) % steps_per_row_block == 0:
            project_rows(idx // steps_per_row_block)


def _attention(x, q, k, v, w_o, sink, g_post, *, layer, seq, tq):
    n, d = x.shape
    dq = q.shape[1]
    dk = k.shape[1]
    tps = seq // tq
    kern = functools.partial(_attn_kernel, seq=seq)
    grid_spec = pltpu.PrefetchScalarGridSpec(
        num_scalar_prefetch=1,
        grid=(n // seq, tps),
        in_specs=[
            pl.BlockSpec((tq, d), lambda b, i, s: (b * tps + i, 0)),
            pl.BlockSpec((tq, dq), lambda b, i, s: (b * tps + i, 0)),
            pl.BlockSpec((seq, dk), lambda b, i, s: (b, 0)),
            pl.BlockSpec((seq // BLOCK, dk, BLOCK), lambda b, i, s: (b, 0, 0)),
            pl.BlockSpec((None, dq, d), lambda b, i, s: (layer, 0, 0), pipeline_mode=pl.Buffered(1)),
            pl.BlockSpec((1, d), lambda b, i, s: (0, 0)),
        ],
        out_specs=pl.BlockSpec((tq, d), lambda b, i, s: (b * tps + i, 0)),
        scratch_shapes=[pltpu.VMEM((tq, dq), BF16)],
    )
    return pl.pallas_call(
        kern,
        name="attention",
        grid_spec=grid_spec,
        out_shape=jax.ShapeDtypeStruct((n, d), F32),
        compiler_params=_params(2),
    )(sink, x, q, k, v, w_o, g_post)


def _rope_tables(seq):
    inv_freq = 1.0 / (ROPE_THETA ** (jnp.arange(0, HEAD_DIM, 2, dtype=F32) / HEAD_DIM))
    ang = jnp.arange(seq, dtype=F32)[:, None] * inv_freq[None, :]
    cos = jnp.cos(ang)
    sin = jnp.sin(ang)
    return jnp.concatenate([cos, cos], axis=-1), jnp.concatenate([-sin, sin], axis=-1)


def _trunk(x3, w, ffn_bf16, *, tm_ffn, tf, tm_conv, tc, tm_qkv, tq):
    b, seq, d = x3.shape
    x = x3.reshape(b * seq, d)
    depth = w["ffn_w_gate"].shape[0]
    cos, sin = _rope_tables(seq)
    for i in range(depth):
        j = i // 2
        g_pre = w["g_mix_pre"][i][None]
        g_post = w["g_mix_post"][i][None]
        if i % 2 == 0:
            x = _conv_mixer(x, g_pre, w["conv_w_in"], w["conv_w_dw"], w["conv_w_out"], g_post,
                            layer=j, seq=seq, tm=tm_conv, tc=tc)
        else:
            q, k, v = _qkv_rope(x, g_pre, w["attn_w_qkv"], cos, sin, layer=j, seq=seq, tm=tm_qkv)
            x = _attention(x, q, k, v, w["attn_w_o"], w["attn_sink"][j], g_post,
                           layer=j, seq=seq, tq=tq)
        cast_next = None
        if i + 1 < depth and i + 1 not in ffn_bf16:
            cast_next = (tuple(w[name] for name in FFN_WEIGHTS), i + 1)
        x, produced = _ffn(x, w["g_ffn_pre"][i][None], *ffn_bf16[i], w["g_ffn_post"][i][None],
                           tm=tm_ffn, tf=tf, cast_next=cast_next)
        if produced is not None:
            ffn_bf16[i + 1] = produced
    return x.reshape(b, seq, d)


def kernel(x_prompt, x_sample, conv_w_in, conv_w_dw, conv_w_out, attn_w_qkv, attn_w_o, attn_sink,
           ffn_w_gate, ffn_w_up, ffn_w_down, g_mix_pre, g_mix_post, g_ffn_pre, g_ffn_post):
    w = dict(
        conv_w_in=conv_w_in.astype(BF16), conv_w_dw=conv_w_dw, conv_w_out=conv_w_out.astype(BF16),
        attn_w_qkv=attn_w_qkv.astype(BF16), attn_w_o=attn_w_o.astype(BF16), attn_sink=attn_sink,
        ffn_w_gate=ffn_w_gate, ffn_w_up=ffn_w_up, ffn_w_down=ffn_w_down,
        g_mix_pre=g_mix_pre, g_mix_post=g_mix_post, g_ffn_pre=g_ffn_pre, g_ffn_post=g_ffn_post,
    )
    ffn_bf16 = {0: tuple(w[name][0].astype(BF16) for name in FFN_WEIGHTS)}
    tiles = dict(tm_ffn=1024, tf=512, tm_conv=1024, tc=512, tm_qkv=512, tq=512)
    y_prompt = _trunk(x_prompt, w, ffn_bf16, **tiles)
    y_sample = _trunk(x_sample, w, ffn_bf16, **tiles)
    return y_prompt, y_sample
```
